```python
import math
import jax, jax.numpy as jnp
from jax import lax
import numpy as np

D_MODEL = 2048
BATCH = 4
SEQ = 2048
DEPTH = 4
DEC_BATCH = 32
DEC_SEQ = 1
PAST_LEN = 16384
PAGE_SIZE = 128

HEAD_DIM = 128
N_HEADS = 12
N_KV_HEADS = 4
GROUP = N_HEADS // N_KV_HEADS
N_MEM_HEADS = 4
N_MEM = 256
D_FF = 5632
MOBA_BLOCK = 256
MOBA_TOPK = 3
MOBA_Q_CHUNK = 32
WINDOW = 128
RMS_EPS = 1e-6
N_MIXERS = 2
N_LAYERS_A = (DEPTH + 1) // 2
N_LAYERS_B = DEPTH // 2
Q_W = N_HEADS * HEAD_DIM
KV_W = N_KV_HEADS * HEAD_DIM
MQ_W = N_MEM_HEADS * HEAD_DIM
IN_W = Q_W + 2 * KV_W + MQ_W
OUT_W = Q_W + MQ_W

kernel_name = 'moba_swa_sink_alibi_macaron_hybrid_step'

F32 = jnp.float32


def _alibi_slope_list(n):
    def pow2(m):
        start = 2.0 ** (-(2.0 ** -(math.log2(m) - 3)))
        return [start ** (i + 1) for i in range(m)]
    if math.log2(n).is_integer():
        return pow2(n)
    c = 2 ** math.floor(math.log2(n))
    return pow2(c) + _alibi_slope_list(2 * c)[0::2][:n - c]


def alibi_slopes(n):
    return jnp.asarray(np.array(_alibi_slope_list(n), dtype=np.float32))


def rms_norm(x, g):
    xf = x.astype(F32)
    y = xf * lax.rsqrt(jnp.mean(xf * xf, axis=-1, keepdims=True) + RMS_EPS)
    return (y * g.astype(F32)).astype(x.dtype)


def swiglu(h, wg, wu, wd):
    return (jax.nn.silu(h @ wg) * (h @ wu)) @ wd


def project_in(hn, w):
    B, S = hn.shape[0], hn.shape[1]
    h = hn @ w
    q = h[..., :Q_W].reshape(B, S, N_HEADS, HEAD_DIM)
    k = h[..., Q_W:Q_W + KV_W].reshape(B, S, N_KV_HEADS, HEAD_DIM)
    v = h[..., Q_W + KV_W:Q_W + 2 * KV_W].reshape(B, S, N_KV_HEADS, HEAD_DIM)
    qm = h[..., Q_W + 2 * KV_W:].reshape(B, S, N_MEM_HEADS, HEAD_DIM)
    return q, k, v, qm


def project_out(o_self, o_mem, w):
    B, S = o_self.shape[0], o_self.shape[1]
    o = jnp.concatenate([o_self.reshape(B, S, Q_W), o_mem.reshape(B, S, MQ_W)], axis=-1)
    return o @ w


def memory_kv(mem, g, w):
    B, M = mem.shape[0], mem.shape[1]
    kv = rms_norm(mem, g) @ w
    return (kv[..., :MQ_W].reshape(B, M, N_MEM_HEADS, HEAD_DIM),
            kv[..., MQ_W:].reshape(B, M, N_MEM_HEADS, HEAD_DIM))


def memory_attention(qm, mk, mv):
    s = jnp.einsum('bqhd,bmhd->bhqm', qm, mk, preferred_element_type=F32) * (HEAD_DIM ** -0.5)
    p = jax.nn.softmax(s, axis=-1).astype(mv.dtype)
    return jnp.einsum('bhqm,bmhd->bqhd', p, mv, preferred_element_type=F32).astype(qm.dtype)


def moba_attention(q, k_past, v_past, k_new, v_new, q_start, slopes):
    B, Q = q.shape[0], q.shape[1]
    L = k_past.shape[1] + k_new.shape[1]
    nb = -(-L // MOBA_BLOCK)
    pad = nb * MOBA_BLOCK - L
    zpad = jnp.zeros((B, pad, N_KV_HEADS, HEAD_DIM), k_new.dtype)
    kb = jnp.concatenate([k_past, k_new, zpad], axis=1).reshape(B, nb, MOBA_BLOCK, N_KV_HEADS, HEAD_DIM)
    vb = jnp.concatenate([v_past, v_new, zpad.astype(v_new.dtype)], axis=1).reshape(B, nb, MOBA_BLOCK, N_KV_HEADS, HEAD_DIM)
    k_mean = jnp.mean(kb, axis=2, dtype=F32)
    n_cand = max(nb, MOBA_TOPK)
    slopes_g = slopes.reshape(N_KV_HEADS, GROUP)
    b_ix = jnp.arange(B)[:, None, None, None, None]
    h_ix = jnp.arange(N_KV_HEADS)[None, None, :, None, None]
    b_own = jnp.arange(B)[:, None]
    offs = jnp.arange(MOBA_BLOCK)
    scale = HEAD_DIM ** -0.5
    n_sel = MOBA_TOPK * MOBA_BLOCK

    def attend(args):
        qc, pc = args
        C = qc.shape[1]
        qg = qc.reshape(B, C, N_KV_HEADS, GROUP, HEAD_DIM)
        own = pc // MOBA_BLOCK
        gate = jnp.einsum('bckgd,bnkd->bckgn', qg.astype(F32), k_mean)
        past = jnp.arange(nb)[None, :] < own[:, None]
        gate = jnp.where(past[None, :, None, None, :], gate, -jnp.inf)
        gate = jnp.pad(gate, ((0, 0), (0, 0), (0, 0), (0, 0), (0, n_cand - nb)), constant_values=-jnp.inf)
        _, sel = lax.top_k(gate, MOBA_TOPK)
        sel_ok = sel < own[None, :, None, None, None]
        sel = jnp.minimum(sel, nb - 1)
        k_sel = kb[b_ix, sel, :, h_ix]
        v_sel = vb[b_ix, sel, :, h_ix]
        s_sel = jnp.einsum('bckgd,bckgtsd->bckgts', qg, k_sel, preferred_element_type=F32) * scale
        dist_sel = pc[None, :, None, None, None, None] - (sel[..., None] * MOBA_BLOCK + offs)
        s_sel = jnp.where(sel_ok[..., None],
                          s_sel - slopes_g[None, None, :, :, None, None] * dist_sel, -jnp.inf)
        k_own = kb[b_own, own[None, :]]
        v_own = vb[b_own, own[None, :]]
        s_own = jnp.einsum('bckgd,bcskd->bckgs', qg, k_own, preferred_element_type=F32) * scale
        dist_own = pc[:, None] - (own[:, None] * MOBA_BLOCK + offs)
        s_own = jnp.where((dist_own >= 0)[None, :, None, None, :],
                          s_own - slopes_g[None, None, :, :, None] * dist_own[None, :, None, None, :], -jnp.inf)
        s = jnp.concatenate([s_sel.reshape(B, C, N_KV_HEADS, GROUP, n_sel), s_own], axis=-1)
        p = jax.nn.softmax(s, axis=-1).astype(v_new.dtype)
        p_sel = p[..., :n_sel].reshape(B, C, N_KV_HEADS, GROUP, MOBA_TOPK, MOBA_BLOCK)
        p_own = p[..., n_sel:]
        o = (jnp.einsum('bckgts,bckgtsd->bckgd', p_sel, v_sel, preferred_element_type=F32)
             + jnp.einsum('bckgs,bcskd->bckgd', p_own, v_own, preferred_element_type=F32))
        return o.reshape(B, C, N_HEADS, HEAD_DIM).astype(qc.dtype)

    C = min(MOBA_Q_CHUNK, Q)
    n = -(-Q // C)
    qpad = jnp.pad(q, ((0, 0), (0, n * C - Q), (0, 0), (0, 0)))
    pos = jnp.pad(q_start + jnp.arange(Q, dtype=jnp.int32), (0, n * C - Q))
    qs = qpad.reshape(B, n, C, N_HEADS, HEAD_DIM).transpose(1, 0, 2, 3, 4)
    out = lax.map(attend, (qs, pos.reshape(n, C)))
    return out.transpose(1, 0, 2, 3, 4).reshape(B, n * C, N_HEADS, HEAD_DIM)[:, :Q]


def swa_attention(q, k_prev, v_prev, k_new, v_new, q_start, slopes, sinks):
    B, Q = q.shape[0], q.shape[1]
    W = WINDOW
    nq = -(-Q // W)
    pad = nq * W - Q
    qb = jnp.pad(q, ((0, 0), (0, pad), (0, 0), (0, 0))).reshape(B, nq, W, N_KV_HEADS, GROUP, HEAD_DIM)
    zpad = jnp.zeros((B, pad, N_KV_HEADS, HEAD_DIM), k_new.dtype)
    kk = jnp.concatenate([k_prev, k_new, zpad], axis=1).reshape(B, nq + 1, W, N_KV_HEADS, HEAD_DIM)
    vv = jnp.concatenate([v_prev, v_new, zpad.astype(v_new.dtype)], axis=1).reshape(B, nq + 1, W, N_KV_HEADS, HEAD_DIM)
    k_band = jnp.concatenate([kk[:, :-1], kk[:, 1:]], axis=2)
    v_band = jnp.concatenate([vv[:, :-1], vv[:, 1:]], axis=2)
    qpos = q_start + jnp.arange(nq * W, dtype=jnp.int32).reshape(nq, W)
    kpos = q_start - W + jnp.arange(nq, dtype=jnp.int32)[:, None] * W + jnp.arange(2 * W, dtype=jnp.int32)[None, :]
    dist = qpos[:, :, None] - kpos[:, None, :]
    ok = (dist >= 0) & (dist <= W) & (kpos[:, None, :] >= 0)
    slopes_g = slopes.reshape(N_KV_HEADS, GROUP)
    s = jnp.einsum('bnqkgd,bnskd->bnkgqs', qb, k_band, preferred_element_type=F32) * (HEAD_DIM ** -0.5)
    s = jnp.where(ok[None, :, None, None], s - slopes_g[None, None, :, :, None, None] * dist[None, :, None, None], -jnp.inf)
    sink = jnp.broadcast_to(sinks.astype(F32).reshape(1, 1, N_KV_HEADS, GROUP, 1, 1), s.shape[:-1] + (1,))
    p = jax.nn.softmax(jnp.concatenate([s, sink], axis=-1), axis=-1)[..., :-1].astype(v_new.dtype)
    o = jnp.einsum('bnkgqs,bnskd->bnqkgd', p, v_band, preferred_element_type=F32)
    return o.reshape(B, nq * W, N_HEADS, HEAD_DIM)[:, :Q].astype(q.dtype)


def setup_inputs(seed: int = 0) -> dict:
    key = jax.random.key(seed)
    keys = iter(jax.random.split(key, 32))

    def nrm(shape, scale=1.0):
        return jax.random.normal(next(keys), shape, F32) * scale

    n_pages = PAST_LEN // PAGE_SIZE
    n_used = DEC_BATCH * n_pages
    n_pool = n_used + n_used // 4
    perm = jax.random.permutation(next(keys), n_pool)
    page_table = perm[:n_used].reshape(DEC_BATCH, n_pages).astype(jnp.int32)
    d_s = D_MODEL ** -0.5
    return {
        'x_prompt': nrm((BATCH, SEQ, D_MODEL)),
        'x_sample': nrm((DEC_BATCH, DEC_SEQ, D_MODEL)),
        'cache_moba_k': nrm((N_LAYERS_A, n_pool, PAGE_SIZE, N_KV_HEADS, HEAD_DIM)),
        'cache_moba_v': nrm((N_LAYERS_A, n_pool, PAGE_SIZE, N_KV_HEADS, HEAD_DIM)),
        'cache_swa_k': nrm((N_LAYERS_B, DEC_BATCH, WINDOW, N_KV_HEADS, HEAD_DIM)),
        'cache_swa_v': nrm((N_LAYERS_B, DEC_BATCH, WINDOW, N_KV_HEADS, HEAD_DIM)),
        'cache_mem_k': nrm((DEPTH, DEC_BATCH, N_MEM, N_MEM_HEADS, HEAD_DIM)),
        'cache_mem_v': nrm((DEPTH, DEC_BATCH, N_MEM, N_MEM_HEADS, HEAD_DIM)),
        'page_table': page_table,
        'mem_prompt': nrm((BATCH, N_MEM, D_MODEL)),
        'g_ffn1': 1.0 + nrm((DEPTH, D_MODEL), 0.02),
        'w_ffn1_gate': nrm((DEPTH, D_MODEL, D_FF), d_s),
        'w_ffn1_up': nrm((DEPTH, D_MODEL, D_FF), d_s),
        'w_ffn1_down': nrm((DEPTH, D_FF, D_MODEL), D_FF ** -0.5),
        'g_attn': 1.0 + nrm((DEPTH, D_MODEL), 0.02),
        'w_in': nrm((DEPTH, D_MODEL, IN_W), d_s),
        'w_out': nrm((DEPTH, OUT_W, D_MODEL), OUT_W ** -0.5),
        'sinks': nrm((N_LAYERS_B, N_HEADS)),
        'g_mem': 1.0 + nrm((DEPTH, D_MODEL), 0.02),
        'w_mem_kv': nrm((DEPTH, D_MODEL, 2 * MQ_W), d_s),
        'g_ffn2': 1.0 + nrm((DEPTH, D_MODEL), 0.02),
        'w_ffn2_gate': nrm((DEPTH, D_MODEL, D_FF), d_s),
        'w_ffn2_up': nrm((DEPTH, D_MODEL, D_FF), d_s),
        'w_ffn2_down': nrm((DEPTH, D_FF, D_MODEL), D_FF ** -0.5),
        'g_final': 1.0 + nrm((D_MODEL,), 0.02),
    }


def reference(x_prompt, x_sample, cache_moba_k, cache_moba_v, cache_swa_k, cache_swa_v,
              cache_mem_k, cache_mem_v, page_table, mem_prompt,
              g_ffn1, w_ffn1_gate, w_ffn1_up, w_ffn1_down,
              g_attn, w_in, w_out, sinks, g_mem, w_mem_kv,
              g_ffn2, w_ffn2_gate, w_ffn2_up, w_ffn2_down, g_final):
    slopes = alibi_slopes(N_HEADS)
    past_len = page_table.shape[1] * PAGE_SIZE
    dec_b = x_sample.shape[0]
    xp, xs = x_prompt, x_sample
    bp = xp.shape[0]
    moba_kp, moba_vp, moba_ks, moba_vs = [], [], [], []
    swa_kp, swa_vp, swa_ks, swa_vs = [], [], [], []
    mem_kp, mem_vp = [], []
    for l in range(DEPTH):
        xp = xp + 0.5 * swiglu(rms_norm(xp, g_ffn1[l]), w_ffn1_gate[l], w_ffn1_up[l], w_ffn1_down[l])
        xs = xs + 0.5 * swiglu(rms_norm(xs, g_ffn1[l]), w_ffn1_gate[l], w_ffn1_up[l], w_ffn1_down[l])
        qp, kp, vp, qmp = project_in(rms_norm(xp, g_attn[l]), w_in[l])
        qs, ks, vs, qms = project_in(rms_norm(xs, g_attn[l]), w_in[l])
        mkp, mvp = memory_kv(mem_prompt, g_mem[l], w_mem_kv[l])
        omp = memory_attention(qmp, mkp, mvp)
        oms = memory_attention(qms, cache_mem_k[l], cache_mem_v[l])
        mem_kp.append(mkp)
        mem_vp.append(mvp)
        j = l // N_MIXERS
        if l % N_MIXERS == 0:
            osp = moba_attention(qp, kp[:, :0], vp[:, :0], kp, vp, 0, slopes)
            past_k = cache_moba_k[j, page_table].reshape(dec_b, past_len, N_KV_HEADS, HEAD_DIM)
            past_v = cache_moba_v[j, page_table].reshape(dec_b, past_len, N_KV_HEADS, HEAD_DIM)
            oss = moba_attention(qs, past_k, past_v, ks, vs, past_len, slopes)
            moba_kp.append(kp)
            moba_vp.append(vp)
            moba_ks.append(ks)
            moba_vs.append(vs)
        else:
            zprev = jnp.zeros((bp, WINDOW, N_KV_HEADS, HEAD_DIM), kp.dtype)
            osp = swa_attention(qp, zprev, zprev, kp, vp, 0, slopes, sinks[j])
            oss = swa_attention(qs, cache_swa_k[j], cache_swa_v[j], ks, vs, past_len, slopes, sinks[j])
            swa_kp.append(jnp.concatenate([zprev, kp], axis=1)[:, -WINDOW:])
            swa_vp.append(jnp.concatenate([zprev.astype(vp.dtype), vp], axis=1)[:, -WINDOW:])
            swa_ks.append(jnp.concatenate([cache_swa_k[j], ks], axis=1)[:, -WINDOW:])
            swa_vs.append(jnp.concatenate([cache_swa_v[j], vs], axis=1)[:, -WINDOW:])
        xp = xp + project_out(osp, omp, w_out[l])
        xs = xs + project_out(oss, oms, w_out[l])
        xp = xp + 0.5 * swiglu(rms_norm(xp, g_ffn2[l]), w_ffn2_gate[l], w_ffn2_up[l], w_ffn2_down[l])
        xs = xs + 0.5 * swiglu(rms_norm(xs, g_ffn2[l]), w_ffn2_gate[l], w_ffn2_up[l], w_ffn2_down[l])
    y_prompt = rms_norm(xp, g_final)
    y_sample = rms_norm(xs, g_final)
    return (y_prompt, y_sample,
            jnp.stack(moba_kp), jnp.stack(moba_vp), jnp.stack(moba_ks), jnp.stack(moba_vs),
            jnp.stack(swa_kp), jnp.stack(swa_vp), jnp.stack(swa_ks), jnp.stack(swa_vs),
            jnp.stack(mem_kp), jnp.stack(mem_vp))
```

```python
import functools
import math

import jax
import jax.numpy as jnp
import numpy as np
from jax import lax
from jax.experimental import pallas as pl
from jax.experimental.pallas import tpu as pltpu

F32 = jnp.float32
BF16 = jnp.bfloat16

D_MODEL = 2048
DEPTH = 4
PAGE_SIZE = 128
HEAD_DIM = 128
N_HEADS = 12
N_KV_HEADS = 4
GROUP = N_HEADS // N_KV_HEADS
N_MEM_HEADS = 4
N_MEM = 256
D_FF = 5632
MOBA_BLOCK = 256
MOBA_TOPK = 3
WINDOW = 128
RMS_EPS = 1e-6
Q_W = N_HEADS * HEAD_DIM
KV_W = N_KV_HEADS * HEAD_DIM
MQ_W = N_MEM_HEADS * HEAD_DIM
IN_W = Q_W + 2 * KV_W + MQ_W
SCALE = HEAD_DIM ** -0.5
PAGES_PER_BLOCK = MOBA_BLOCK // PAGE_SIZE

VMEM_LIMIT_BYTES = 56 * 1024 * 1024
MASKED = -1e30
Q_TILE = 256
GATE_PAGES_PER_STEP = 16


def _alibi_slope_list(n):
    def pow2(m):
        start = 2.0 ** (-(2.0 ** -(math.log2(m) - 3)))
        return [start ** (i + 1) for i in range(m)]
    if math.log2(n).is_integer():
        return pow2(n)
    c = 2 ** math.floor(math.log2(n))
    return pow2(c) + _alibi_slope_list(2 * c)[0::2][:n - c]


SLOPES = np.array(_alibi_slope_list(N_HEADS), dtype=np.float32)


def _params(n_grid_dims):
    return pltpu.CompilerParams(
        dimension_semantics=("arbitrary",) * n_grid_dims,
        vmem_limit_bytes=VMEM_LIMIT_BYTES)


def _dot_t(a, b, precision=None):
    return lax.dot_general(a, b, (((1,), (1,)), ((), ())), precision=precision,
                           preferred_element_type=F32)


def _norm_kernel(x_ref, g_ref, o_ref):
    x = x_ref[...]
    ms = jnp.mean(x * x, axis=-1, keepdims=True)
    o_ref[...] = (x * lax.rsqrt(ms + RMS_EPS) * g_ref[...]).astype(o_ref.dtype)


def rms_norm(x, g_all, layer, out_dtype):
    m, d = x.shape
    tm = min(m, 512)
    g3 = g_all.reshape(g_all.shape[0], 1, d)
    return pl.pallas_call(
        _norm_kernel,
        grid=(m // tm,),
        in_specs=[pl.BlockSpec((tm, d), lambda i: (i, 0)),
                  pl.BlockSpec((None, 1, d), lambda i: (layer, 0, 0))],
        out_specs=pl.BlockSpec((tm, d), lambda i: (i, 0)),
        out_shape=jax.ShapeDtypeStruct((m, d), out_dtype),
        compiler_params=_params(1),
        name="rms_norm",
    )(x, g3)


def _proj_kernel(h_ref, w_ref, o_ref, wb_ref):
    @pl.when(pl.program_id(1) == 0)
    def _():
        wb_ref[...] = w_ref[...].astype(BF16)
    o_ref[...] = jnp.dot(h_ref[...], wb_ref[...], preferred_element_type=F32)


def project(h, w_all, layer, tn):
    m, k = h.shape
    n = w_all.shape[2]
    tm = min(m, 1024)
    return pl.pallas_call(
        _proj_kernel,
        grid=(n // tn, m // tm),
        in_specs=[pl.BlockSpec((tm, k), lambda j, i: (i, 0)),
                  pl.BlockSpec((None, k, tn), lambda j, i: (layer, 0, j))],
        out_specs=pl.BlockSpec((tm, tn), lambda j, i: (i, j)),
        out_shape=jax.ShapeDtypeStruct((m, n), F32),
        scratch_shapes=[pltpu.VMEM((k, tn), BF16)],
        compiler_params=_params(2),
        name="project",
    )(h, w_all)


def _ffn_up_kernel(h_ref, wg_ref, wu_ref, a_ref, wgb_ref, wub_ref):
    @pl.when(pl.program_id(1) == 0)
    def _():
        wgb_ref[...] = wg_ref[...].astype(BF16)
        wub_ref[...] = wu_ref[...].astype(BF16)
    h = h_ref[...]
    g = jnp.dot(h, wgb_ref[...], preferred_element_type=F32)
    u = jnp.dot(h, wub_ref[...], preferred_element_type=F32)
    a_ref[...] = (g * jax.nn.sigmoid(g) * u).astype(a_ref.dtype)


def ffn_up(h, wg_all, wu_all, layer):
    m, k = h.shape
    n = wg_all.shape[2]
    tm = min(m, 1024)
    tn = 512
    w_spec = pl.BlockSpec((None, k, tn), lambda j, i: (layer, 0, j))
    return pl.pallas_call(
        _ffn_up_kernel,
        grid=(n // tn, m // tm),
        in_specs=[pl.BlockSpec((tm, k), lambda j, i: (i, 0)), w_spec, w_spec],
        out_specs=pl.BlockSpec((tm, tn), lambda j, i: (i, j)),
        out_shape=jax.ShapeDtypeStruct((m, n), BF16),
        scratch_shapes=[pltpu.VMEM((k, tn), BF16), pltpu.VMEM((k, tn), BF16)],
        compiler_params=_params(2),
        name="ffn_up",
    )(h, wg_all, wu_all)


def _mm_res_kernel(*refs, k_sizes, scale):
    n_a = len(k_sizes)
    a_refs = refs[:n_a]
    w_ref, x_ref, o_ref, wb_ref = refs[n_a:]

    @pl.when(pl.program_id(1) == 0)
    def _():
        wb_ref[...] = w_ref[...].astype(BF16)

    acc = None
    off = 0
    for a_ref, ksz in zip(a_refs, k_sizes):
        d = jnp.dot(a_ref[...].astype(BF16), wb_ref[off:off + ksz, :],
                    preferred_element_type=F32)
        acc = d if acc is None else acc + d
        off += ksz
    o_ref[...] = x_ref[...] + scale * acc


def matmul_residual(a_list, w_all, layer, x, scale, tm, tn):
    m, n = x.shape
    k_sizes = tuple(a.shape[1] for a in a_list)
    k = sum(k_sizes)
    tm = min(m, tm)
    a_specs = [pl.BlockSpec((tm, ks), lambda j, i: (i, 0)) for ks in k_sizes]
    return pl.pallas_call(
        functools.partial(_mm_res_kernel, k_sizes=k_sizes, scale=scale),
        grid=(n // tn, m // tm),
        in_specs=a_specs + [pl.BlockSpec((None, k, tn), lambda j, i: (layer, 0, j)),
                            pl.BlockSpec((tm, tn), lambda j, i: (i, j))],
        out_specs=pl.BlockSpec((tm, tn), lambda j, i: (i, j)),
        out_shape=jax.ShapeDtypeStruct((m, n), F32),
        scratch_shapes=[pltpu.VMEM((k, tn), BF16)],
        compiler_params=_params(2),
        name="matmul_residual",
    )(*a_list, w_all, x)


def _softmax_pv(s, v):
    m = jnp.max(s, axis=-1, keepdims=True)
    p = jnp.exp(s - m)
    l = jnp.sum(p, axis=-1, keepdims=True)
    return jnp.dot(p.astype(BF16), v, preferred_element_type=F32), m, l


def _memory_attention_tile(qm_ref, mk_ref, mv_ref, om_ref):
    qm = (qm_ref[...] * SCALE).astype(BF16)
    s = _dot_t(qm, mk_ref[...].astype(BF16))
    o, _, l = _softmax_pv(s, mv_ref[...].astype(BF16))
    om_ref[...] = (o / l).astype(om_ref.dtype)


def _moba_prompt_kernel(slopes_ref, q_ref, k_ref, v_ref, qm_ref, mk_ref, mv_ref,
                        os_ref, om_ref, kb_ref, vb_ref, kmean_ref):
    kvh = pl.program_id(1)
    i = pl.program_id(2)
    nb = kmean_ref.shape[0]
    tq = q_ref.shape[0]

    @pl.when(i == 0)
    def _():
        kb_ref[...] = k_ref[...].astype(BF16)
        vb_ref[...] = v_ref[...].astype(BF16)
        for n in range(nb):
            kmean_ref[n:n + 1, :] = jnp.mean(
                k_ref[n * MOBA_BLOCK:(n + 1) * MOBA_BLOCK, :], axis=0, keepdims=True)

    row = lax.broadcasted_iota(jnp.int32, (tq, MOBA_BLOCK), 0)
    col = lax.broadcasted_iota(jnp.int32, (tq, MOBA_BLOCK), 1)
    dist_own = (row - col).astype(F32)
    causal = row >= col
    blk = lax.broadcasted_iota(jnp.int32, (tq, nb), 1)
    own_start = pl.multiple_of(i * MOBA_BLOCK, MOBA_BLOCK)

    for g in range(GROUP):
        slope = slopes_ref[kvh * GROUP + g]
        q = q_ref[:, g * HEAD_DIM:(g + 1) * HEAD_DIM]
        gate = _dot_t(q, kmean_ref[...], precision=lax.Precision.HIGHEST)
        rank = jnp.zeros((tq, nb), jnp.int32)
        for m in range(nb - 1):
            gm = gate[:, m:m + 1]
            beats = (gm > gate) | ((gm == gate) & (m < blk))
            rank = rank + jnp.where(beats & (m < i), 1, 0)
        chosen = jnp.where((blk < i) & (rank < MOBA_TOPK), 1.0, 0.0)

        qb = (q * SCALE).astype(BF16)
        s = _dot_t(qb, kb_ref[pl.ds(own_start, MOBA_BLOCK), :]) - slope * dist_own
        s = jnp.where(causal, s, MASKED)
        acc, m_run, l_run = _softmax_pv(s, vb_ref[pl.ds(own_start, MOBA_BLOCK), :])

        def past_block(n, carry):
            acc, m_run, l_run = carry
            start = pl.multiple_of(n * MOBA_BLOCK, MOBA_BLOCK)
            picked = jnp.sum(jnp.where(blk == n, chosen, 0.0), axis=-1, keepdims=True)
            offset = ((i - n) * MOBA_BLOCK).astype(F32)
            s = _dot_t(qb, kb_ref[pl.ds(start, MOBA_BLOCK), :]) - slope * (dist_own + offset)
            s = jnp.where(picked > 0.0, s, MASKED)
            m_new = jnp.maximum(m_run, jnp.max(s, axis=-1, keepdims=True))
            alpha = jnp.exp(m_run - m_new)
            p = jnp.exp(s - m_new)
            l_new = alpha * l_run + jnp.sum(p, axis=-1, keepdims=True)
            pv = jnp.dot(p.astype(BF16), vb_ref[pl.ds(start, MOBA_BLOCK), :],
                         preferred_element_type=F32)
            return alpha * acc + pv, m_new, l_new

        acc, m_run, l_run = lax.fori_loop(0, i, past_block, (acc, m_run, l_run))
        os_ref[:, g * HEAD_DIM:(g + 1) * HEAD_DIM] = (acc / l_run).astype(os_ref.dtype)

    _memory_attention_tile(qm_ref, mk_ref, mv_ref, om_ref)


def _swa_prompt_kernel(slopes_ref, sinks_ref, q_ref, k_ref, v_ref, qm_ref, mk_ref, mv_ref,
                       os_ref, om_ref, kb_ref, vb_ref):
    kvh = pl.program_id(1)
    i = pl.program_id(2)
    tq = q_ref.shape[0]
    span = tq + WINDOW

    @pl.when(i == 0)
    def _():
        kb_ref[...] = k_ref[...].astype(BF16)
        vb_ref[...] = v_ref[...].astype(BF16)

    kstart = pl.multiple_of(jnp.maximum(i * tq - WINDOW, 0), WINDOW)
    row = lax.broadcasted_iota(jnp.int32, (tq, span), 0)
    col = lax.broadcasted_iota(jnp.int32, (tq, span), 1)
    dist = row - col + (i * tq - kstart)
    visible = (dist >= 0) & (dist <= WINDOW)
    dist_f = dist.astype(F32)
    kw = kb_ref[pl.ds(kstart, span), :]
    vw = vb_ref[pl.ds(kstart, span), :]

    for g in range(GROUP):
        slope = slopes_ref[kvh * GROUP + g]
        sink = sinks_ref[kvh * GROUP + g]
        qb = (q_ref[:, g * HEAD_DIM:(g + 1) * HEAD_DIM] * SCALE).astype(BF16)
        s = jnp.where(visible, _dot_t(qb, kw) - slope * dist_f, MASKED)
        m = jnp.maximum(jnp.max(s, axis=-1, keepdims=True), sink)
        p = jnp.exp(s - m)
        l = jnp.sum(p, axis=-1, keepdims=True) + jnp.exp(sink - m)
        o = jnp.dot(p.astype(BF16), vw, preferred_element_type=F32)
        os_ref[:, g * HEAD_DIM:(g + 1) * HEAD_DIM] = (o / l).astype(os_ref.dtype)

    _memory_attention_tile(qm_ref, mk_ref, mv_ref, om_ref)


def prompt_attention(qkv, memkv, batch, seq, sinks_layer):
    nq = seq // Q_TILE
    k_col = Q_W // HEAD_DIM
    v_col = (Q_W + KV_W) // HEAD_DIM
    qm_col = (Q_W + 2 * KV_W) // HEAD_DIM
    smem = pl.BlockSpec(memory_space=pltpu.SMEM)
    tensor_specs = [
        pl.BlockSpec((Q_TILE, GROUP * HEAD_DIM), lambda b, h, i: (b * nq + i, h)),
        pl.BlockSpec((seq, HEAD_DIM), lambda b, h, i: (b, k_col + h)),
        pl.BlockSpec((seq, HEAD_DIM), lambda b, h, i: (b, v_col + h)),
        pl.BlockSpec((Q_TILE, HEAD_DIM), lambda b, h, i: (b * nq + i, qm_col + h)),
        pl.BlockSpec((N_MEM, HEAD_DIM), lambda b, h, i: (b, h)),
        pl.BlockSpec((N_MEM, HEAD_DIM), lambda b, h, i: (b, N_MEM_HEADS + h)),
    ]
    out_specs = [
        pl.BlockSpec((Q_TILE, GROUP * HEAD_DIM), lambda b, h, i: (b * nq + i, h)),
        pl.BlockSpec((Q_TILE, HEAD_DIM), lambda b, h, i: (b * nq + i, h)),
    ]
    out_shape = [jax.ShapeDtypeStruct((batch * seq, Q_W), BF16),
                 jax.ShapeDtypeStruct((batch * seq, MQ_W), BF16)]
    scratch = [pltpu.VMEM((seq, HEAD_DIM), BF16), pltpu.VMEM((seq, HEAD_DIM), BF16)]
    slopes = jnp.asarray(SLOPES)
    tensors = (qkv, qkv, qkv, qkv, memkv, memkv)
    if sinks_layer is None:
        return pl.pallas_call(
            _moba_prompt_kernel,
            grid=(batch, N_KV_HEADS, nq),
            in_specs=[smem] + tensor_specs,
            out_specs=out_specs, out_shape=out_shape,
            scratch_shapes=scratch + [pltpu.VMEM((seq // MOBA_BLOCK, HEAD_DIM), F32)],
            compiler_params=_params(3),
            name="moba_prompt",
        )(slopes, *tensors)
    return pl.pallas_call(
        _swa_prompt_kernel,
        grid=(batch, N_KV_HEADS, nq),
        in_specs=[smem, smem] + tensor_specs,
        out_specs=out_specs, out_shape=out_shape,
        scratch_shapes=scratch,
        compiler_params=_params(3),
        name="swa_prompt",
    )(slopes, sinks_layer, *tensors)


def _row_scores(k, q_row):
    return jnp.sum(k * q_row, axis=-1, keepdims=True) * SCALE


def _sample_memory_kernel(qm_ref, mk_ref, mv_ref, o_ref):
    for h in range(N_MEM_HEADS):
        lanes = slice(h * HEAD_DIM, (h + 1) * HEAD_DIM)
        s = _row_scores(mk_ref[:, lanes], qm_ref[:, lanes])
        p = jnp.exp(s - jnp.max(s, axis=0, keepdims=True))
        l = jnp.sum(p, axis=0, keepdims=True)
        o_ref[:, lanes] = jnp.sum(p * mv_ref[:, lanes], axis=0, keepdims=True) / l


def sample_memory_attention(qm, mem_k, mem_v, layer):
    b = qm.shape[0]
    row = pl.BlockSpec((None, 1, MQ_W), lambda i: (i, 0, 0))
    cache = pl.BlockSpec((None, None, N_MEM, MQ_W), lambda i: (layer, i, 0, 0))
    return pl.pallas_call(
        _sample_memory_kernel,
        grid=(b,),
        in_specs=[row, cache, cache],
        out_specs=row,
        out_shape=jax.ShapeDtypeStruct((b, 1, MQ_W), F32),
        compiler_params=_params(1),
        name="sample_memory",
    )(qm, mem_k, mem_v)


def _sample_swa_kernel(sinks_ref, q_ref, kc_ref, vc_ref, kn_ref, vn_ref, o_ref):
    back = (WINDOW - lax.broadcasted_iota(jnp.int32, (WINDOW, 1), 0)).astype(F32)
    for hq in range(N_HEADS):
        kv = hq // GROUP
        q_row = q_ref[:, hq * HEAD_DIM:(hq + 1) * HEAD_DIM]
        lanes = slice(kv * HEAD_DIM, (kv + 1) * HEAD_DIM)
        sink = sinks_ref[hq]
        s_c = _row_scores(kc_ref[:, lanes], q_row) - float(SLOPES[hq]) * back
        s_n = _row_scores(kn_ref[:, lanes], q_row)
        m = jnp.maximum(jnp.maximum(jnp.max(s_c, axis=0, keepdims=True), s_n), sink)
        p_c = jnp.exp(s_c - m)
        p_n = jnp.exp(s_n - m)
        l = jnp.sum(p_c, axis=0, keepdims=True) + p_n + jnp.exp(sink - m)
        o = jnp.sum(p_c * vc_ref[:, lanes], axis=0, keepdims=True) + p_n * vn_ref[:, lanes]
        o_ref[:, hq * HEAD_DIM:(hq + 1) * HEAD_DIM] = o / l


def sample_swa_attention(q, k_new, v_new, cache_k, cache_v, sinks_layer, layer_b):
    b = q.shape[0]
    q_spec = pl.BlockSpec((None, 1, Q_W), lambda i: (i, 0, 0))
    kv_row = pl.BlockSpec((None, 1, KV_W), lambda i: (i, 0, 0))
    cache = pl.BlockSpec((None, None, WINDOW, KV_W), lambda i: (layer_b, i, 0, 0))
    return pl.pallas_call(
        _sample_swa_kernel,
        grid=(b,),
        in_specs=[pl.BlockSpec(memory_space=pltpu.SMEM), q_spec, cache, cache, kv_row, kv_row],
        out_specs=q_spec,
        out_shape=jax.ShapeDtypeStruct((b, 1, Q_W), F32),
        compiler_params=_params(1),
        name="sample_swa",
    )(sinks_layer, q, cache_k, cache_v, k_new, v_new)


def _moba_gate_kernel(pt_ref, q_ref, *refs):
    del pt_ref
    pages = refs[:GATE_PAGES_PER_STEP]
    sel_ref, kmean_ref = refs[GATE_PAGES_PER_STEP:]
    s = pl.program_id(1)
    blocks_per_step = GATE_PAGES_PER_STEP // PAGES_PER_BLOCK
    n_blocks = kmean_ref.shape[0]

    sub = lax.broadcasted_iota(jnp.int32, (blocks_per_step, KV_W), 0)
    means = jnp.zeros((blocks_per_step, KV_W), F32)
    for r in range(blocks_per_step):
        total = None
        for page in pages[r * PAGES_PER_BLOCK:(r + 1) * PAGES_PER_BLOCK]:
            part = jnp.sum(page[...], axis=0, keepdims=True)
            total = part if total is None else total + part
        means = jnp.where(sub == r, total * (1.0 / MOBA_BLOCK), means)
    kmean_ref[pl.ds(pl.multiple_of(s * blocks_per_step, blocks_per_step), blocks_per_step), :] = means

    @pl.when(s == pl.num_programs(1) - 1)
    def _():
        q = q_ref[...]
        head = lax.broadcasted_iota(jnp.int32, (N_HEADS, n_blocks), 0)
        gate = jnp.zeros((N_HEADS, n_blocks), F32)
        for kv in range(N_KV_HEADS):
            g_kv = _dot_t(q, kmean_ref[:, kv * HEAD_DIM:(kv + 1) * HEAD_DIM],
                          precision=lax.Precision.HIGHEST)
            gate = jnp.where((head >= kv * GROUP) & (head < (kv + 1) * GROUP), g_kv, gate)
        blk = lax.broadcasted_iota(jnp.int32, (N_HEADS, n_blocks), 1).astype(F32)
        lane = lax.broadcasted_iota(jnp.int32, sel_ref.shape, 1)
        picks = jnp.zeros(sel_ref.shape, F32)
        for t in range(MOBA_TOPK):
            best = jnp.max(gate, axis=-1, keepdims=True)
            idx = jnp.min(jnp.where(gate == best, blk, float(n_blocks)), axis=-1, keepdims=True)
            picks = jnp.where(lane == t, idx, picks)
            gate = jnp.where(blk == idx, -jnp.inf, gate)
        sel_ref[...] = picks.astype(jnp.int32)


def sample_moba_select(q, cache_k, page_table, layer_a):
    b = q.shape[0]
    n_pages = page_table.shape[1]
    n_blocks = n_pages // PAGES_PER_BLOCK
    steps = n_pages // GATE_PAGES_PER_STEP

    def page_spec(r):
        return pl.BlockSpec(
            (None, None, PAGE_SIZE, KV_W),
            lambda i, s, pt: (layer_a, pt[i, s * GATE_PAGES_PER_STEP + r], 0, 0))

    grid_spec = pltpu.PrefetchScalarGridSpec(
        num_scalar_prefetch=1,
        grid=(b, steps),
        in_specs=[pl.BlockSpec((None, N_HEADS, HEAD_DIM), lambda i, s, pt: (i, 0, 0))]
        + [page_spec(r) for r in range(GATE_PAGES_PER_STEP)],
        out_specs=pl.BlockSpec((None, N_HEADS, HEAD_DIM), lambda i, s, pt: (i, 0, 0)),
        scratch_shapes=[pltpu.VMEM((n_blocks, KV_W), F32)],
    )
    sel = pl.pallas_call(
        _moba_gate_kernel,
        grid_spec=grid_spec,
        out_shape=jax.ShapeDtypeStruct((b, N_HEADS, HEAD_DIM), jnp.int32),
        compiler_params=_params(2),
        name="sample_moba_select",
    )(page_table, q, *([cache_k] * GATE_PAGES_PER_STEP))
    return sel[:, :, :MOBA_TOPK]


N_SEL_PAGES = GROUP * MOBA_TOPK * PAGES_PER_BLOCK


def _moba_sample_kernel(pt_ref, sel_ref, slopes_ref, q_ref, kn_ref, vn_ref, *refs, past_len):
    del pt_ref
    k_pages = refs[:N_SEL_PAGES]
    v_pages = refs[N_SEL_PAGES:2 * N_SEL_PAGES]
    o_ref = refs[2 * N_SEL_PAGES]
    b = pl.program_id(0)
    kvh = pl.program_id(1)
    within = lax.broadcasted_iota(jnp.int32, (PAGE_SIZE, 1), 0)
    for g in range(GROUP):
        head = kvh * GROUP + g
        slope = slopes_ref[head]
        q_row = q_ref[g:g + 1, :]
        s_own = _row_scores(kn_ref[...], q_row)
        scores = []
        m = s_own
        for t in range(MOBA_TOPK):
            first = sel_ref[(b * N_HEADS + head) * MOBA_TOPK + t] * MOBA_BLOCK
            for r in range(PAGES_PER_BLOCK):
                page = k_pages[(g * MOBA_TOPK + t) * PAGES_PER_BLOCK + r]
                dist = (past_len - (first + r * PAGE_SIZE) - within).astype(F32)
                s = _row_scores(page[...], q_row) - slope * dist
                scores.append(s)
                m = jnp.maximum(m, jnp.max(s, axis=0, keepdims=True))
        p_own = jnp.exp(s_own - m)
        l = p_own
        o = p_own * vn_ref[...]
        for idx, s in enumerate(scores):
            p = jnp.exp(s - m)
            l = l + jnp.sum(p, axis=0, keepdims=True)
            o = o + jnp.sum(p * v_pages[g * MOBA_TOPK * PAGES_PER_BLOCK + idx][...],
                            axis=0, keepdims=True)
        o_ref[g:g + 1, :] = o / l


def sample_moba_attention(q, k_new, v_new, cache_k, cache_v, page_table, sel, layer_a):
    b = q.shape[0]
    past_len = page_table.shape[1] * PAGE_SIZE

    def page_spec(g, t, r):
        def index_map(i, h, pt, sl):
            block = sl[(i * N_HEADS + h * GROUP + g) * MOBA_TOPK + t]
            return (layer_a, pt[i, block * PAGES_PER_BLOCK + r], 0, h)
        return pl.BlockSpec((None, None, PAGE_SIZE, HEAD_DIM), index_map)

    page_specs = [page_spec(g, t, r) for g in range(GROUP) for t in range(MOBA_TOPK)
                  for r in range(PAGES_PER_BLOCK)]
    q_spec = pl.BlockSpec((None, None, GROUP, HEAD_DIM), lambda i, h, pt, sl: (i, h, 0, 0))
    row_spec = pl.BlockSpec((None, None, 1, HEAD_DIM), lambda i, h, pt, sl: (i, h, 0, 0))
    grid_spec = pltpu.PrefetchScalarGridSpec(
        num_scalar_prefetch=2,
        grid=(b, N_KV_HEADS),
        in_specs=[pl.BlockSpec(memory_space=pltpu.SMEM), q_spec, row_spec, row_spec]
        + page_specs + page_specs,
        out_specs=q_spec,
    )
    return pl.pallas_call(
        functools.partial(_moba_sample_kernel, past_len=past_len),
        grid_spec=grid_spec,
        out_shape=jax.ShapeDtypeStruct((b, N_KV_HEADS, GROUP, HEAD_DIM), F32),
        compiler_params=_params(2),
        name="sample_moba",
    )(page_table, sel.reshape(-1), jnp.asarray(SLOPES), q, k_new, v_new,
      *([cache_k] * N_SEL_PAGES), *([cache_v] * N_SEL_PAGES))


def _half_ffn(x, g_all, wg_all, wu_all, wd_all, layer):
    h = rms_norm(x, g_all, layer, BF16)
    a = ffn_up(h, wg_all, wu_all, layer)
    return matmul_residual([a], wd_all, layer, x, 0.5, tm=512, tn=512)


def kernel(x_prompt, x_sample, cache_moba_k, cache_moba_v, cache_swa_k, cache_swa_v, cache_mem_k, cache_mem_v, page_table, mem_prompt, g_ffn1, w_ffn1_gate, w_ffn1_up, w_ffn1_down, g_attn, w_in, w_out, sinks, g_mem, w_mem_kv, g_ffn2, w_ffn2_gate, w_ffn2_up, w_ffn2_down, g_final):
    bp, seq, d = x_prompt.shape
    bs = x_sample.shape[0]
    n_pool = cache_moba_k.shape[1]
    xp = x_prompt.reshape(bp * seq, d)
    xs = x_sample.reshape(bs, d)
    mem = mem_prompt.reshape(bp * N_MEM, d)
    mem_k_cache = cache_mem_k.reshape(DEPTH, bs, N_MEM, MQ_W)
    mem_v_cache = cache_mem_v.reshape(DEPTH, bs, N_MEM, MQ_W)
    swa_k_cache = cache_swa_k.reshape(-1, bs, WINDOW, KV_W)
    swa_v_cache = cache_swa_v.reshape(-1, bs, WINDOW, KV_W)
    moba_k_flat = cache_moba_k.reshape(-1, n_pool, PAGE_SIZE, KV_W)
    moba_v_flat = cache_moba_v.reshape(-1, n_pool, PAGE_SIZE, KV_W)

    moba_kp, moba_vp, moba_ks, moba_vs = [], [], [], []
    swa_kp, swa_vp, swa_ks, swa_vs = [], [], [], []
    mem_kp, mem_vp = [], []
    for l in range(DEPTH):
        xp = _half_ffn(xp, g_ffn1, w_ffn1_gate, w_ffn1_up, w_ffn1_down, l)
        xs = _half_ffn(xs, g_ffn1, w_ffn1_gate, w_ffn1_up, w_ffn1_down, l)

        qkv_p = project(rms_norm(xp, g_attn, l, BF16), w_in, l, tn=1024)
        qkv_s = project(rms_norm(xs, g_attn, l, BF16), w_in, l, tn=1024)
        memkv = project(rms_norm(mem, g_mem, l, BF16), w_mem_kv, l, tn=512)
        mem_kp.append(memkv[:, :MQ_W].reshape(bp, N_MEM, N_MEM_HEADS, HEAD_DIM))
        mem_vp.append(memkv[:, MQ_W:].reshape(bp, N_MEM, N_MEM_HEADS, HEAD_DIM))

        kp = qkv_p[:, Q_W:Q_W + KV_W].reshape(bp, seq, N_KV_HEADS, HEAD_DIM)
        vp = qkv_p[:, Q_W + KV_W:Q_W + 2 * KV_W].reshape(bp, seq, N_KV_HEADS, HEAD_DIM)
        qs = qkv_s[:, :Q_W]
        ks = qkv_s[:, Q_W:Q_W + KV_W]
        vs = qkv_s[:, Q_W + KV_W:Q_W + 2 * KV_W]
        qms = qkv_s[:, Q_W + 2 * KV_W:]
        ks4 = ks.reshape(bs, 1, N_KV_HEADS, HEAD_DIM)
        vs4 = vs.reshape(bs, 1, N_KV_HEADS, HEAD_DIM)

        oms = sample_memory_attention(qms.reshape(bs, 1, MQ_W), mem_k_cache, mem_v_cache, l)
        j = l // 2
        if l % 2 == 0:
            osp, omp = prompt_attention(qkv_p, memkv, bp, seq, None)
            sel = sample_moba_select(qs.reshape(bs, N_HEADS, HEAD_DIM), moba_k_flat, page_table, j)
            oss = sample_moba_attention(
                qs.reshape(bs, N_KV_HEADS, GROUP, HEAD_DIM),
                ks.reshape(bs, N_KV_HEADS, 1, HEAD_DIM), vs.reshape(bs, N_KV_HEADS, 1, HEAD_DIM),
                moba_k_flat, moba_v_flat, page_table, sel, j)
            moba_kp.append(kp)
            moba_vp.append(vp)
            moba_ks.append(ks4)
            moba_vs.append(vs4)
        else:
            osp, omp = prompt_attention(qkv_p, memkv, bp, seq, sinks[j])
            oss = sample_swa_attention(qs.reshape(bs, 1, Q_W), ks.reshape(bs, 1, KV_W),
                                       vs.reshape(bs, 1, KV_W), swa_k_cache, swa_v_cache,
                                       sinks[j], j)
            swa_kp.append(kp[:, -WINDOW:])
            swa_vp.append(vp[:, -WINDOW:])
            swa_ks.append(jnp.concatenate([cache_swa_k[j], ks4], axis=1)[:, -WINDOW:])
            swa_vs.append(jnp.concatenate([cache_swa_v[j], vs4], axis=1)[:, -WINDOW:])

        xp = matmul_residual([osp, omp], w_out, l, xp, 1.0, tm=1024, tn=1024)
        xs = matmul_residual([oss.reshape(bs, Q_W), oms.reshape(bs, MQ_W)], w_out, l, xs, 1.0,
                             tm=1024, tn=1024)

        xp = _half_ffn(xp, g_ffn2, w_ffn2_gate, w_ffn2_up, w_ffn2_down, l)
        xs = _half_ffn(xs, g_ffn2, w_ffn2_gate, w_ffn2_up, w_ffn2_down, l)

    g_fin = g_final.reshape(1, d)
    y_prompt = rms_norm(xp, g_fin, 0, F32).reshape(bp, seq, d)
    y_sample = rms_norm(xs, g_fin, 0, F32).reshape(bs, 1, d)
    return (y_prompt, y_sample,
            jnp.stack(moba_kp), jnp.stack(moba_vp), jnp.stack(moba_ks), jnp.stack(moba_vs),
            jnp.stack(swa_kp), jnp.stack(swa_vp), jnp.stack(swa_ks), jnp.stack(swa_vs),
            jnp.stack(mem_kp), jnp.stack(mem_vp))
```

```python
import functools
import math

import jax
import jax.numpy as jnp
import numpy as np
from jax import lax
from jax.experimental import pallas as pl
from jax.experimental.pallas import tpu as pltpu

F32 = jnp.float32
BF16 = jnp.bfloat16

D_MODEL = 2048
DEPTH = 4
PAGE_SIZE = 128
HEAD_DIM = 128
N_HEADS = 12
N_KV_HEADS = 4
GROUP = N_HEADS // N_KV_HEADS
N_MEM_HEADS = 4
N_MEM = 256
D_FF = 5632
MOBA_BLOCK = 256
MOBA_TOPK = 3
WINDOW = 128
RMS_EPS = 1e-6
Q_W = N_HEADS * HEAD_DIM
KV_W = N_KV_HEADS * HEAD_DIM
MQ_W = N_MEM_HEADS * HEAD_DIM
IN_W = Q_W + 2 * KV_W + MQ_W
SCALE = HEAD_DIM ** -0.5
PAGES_PER_BLOCK = MOBA_BLOCK // PAGE_SIZE

VMEM_LIMIT_BYTES = 56 * 1024 * 1024
MASKED = -1e30
Q_TILE = 256
GATE_PAGES_PER_STEP = 16


def _alibi_slope_list(n):
    def pow2(m):
        start = 2.0 ** (-(2.0 ** -(math.log2(m) - 3)))
        return [start ** (i + 1) for i in range(m)]
    if math.log2(n).is_integer():
        return pow2(n)
    c = 2 ** math.floor(math.log2(n))
    return pow2(c) + _alibi_slope_list(2 * c)[0::2][:n - c]


SLOPES = np.array(_alibi_slope_list(N_HEADS), dtype=np.float32)


def _params(n_grid_dims):
    return pltpu.CompilerParams(
        dimension_semantics=("arbitrary",) * n_grid_dims,
        vmem_limit_bytes=VMEM_LIMIT_BYTES)


def _dot_t(a, b, precision=None):
    return lax.dot_general(a, b, (((1,), (1,)), ((), ())), precision=precision,
                           preferred_element_type=F32)


def _norm_kernel(x_ref, g_ref, o_ref):
    x = x_ref[...]
    ms = jnp.mean(x * x, axis=-1, keepdims=True)
    o_ref[...] = (x * lax.rsqrt(ms + RMS_EPS) * g_ref[...]).astype(o_ref.dtype)


def rms_norm(x, g_all, layer, out_dtype):
    m, d = x.shape
    tm = min(m, 512)
    g3 = g_all.reshape(g_all.shape[0], 1, d)
    return pl.pallas_call(
        _norm_kernel,
        grid=(m // tm,),
        in_specs=[pl.BlockSpec((tm, d), lambda i: (i, 0)),
                  pl.BlockSpec((None, 1, d), lambda i: (layer, 0, 0))],
        out_specs=pl.BlockSpec((tm, d), lambda i: (i, 0)),
        out_shape=jax.ShapeDtypeStruct((m, d), out_dtype),
        compiler_params=_params(1),
        name="rms_norm",
    )(x, g3)


def _proj_kernel(h_ref, w_ref, o_ref, wb_ref):
    @pl.when(pl.program_id(1) == 0)
    def _():
        wb_ref[...] = w_ref[...].astype(BF16)
    o_ref[...] = jnp.dot(h_ref[...], wb_ref[...], preferred_element_type=F32)


def project(h, w_all, layer, tn):
    m, k = h.shape
    n = w_all.shape[2]
    tm = min(m, 1024)
    return pl.pallas_call(
        _proj_kernel,
        grid=(n // tn, m // tm),
        in_specs=[pl.BlockSpec((tm, k), lambda j, i: (i, 0)),
                  pl.BlockSpec((None, k, tn), lambda j, i: (layer, 0, j))],
        out_specs=pl.BlockSpec((tm, tn), lambda j, i: (i, j)),
        out_shape=jax.ShapeDtypeStruct((m, n), F32),
        scratch_shapes=[pltpu.VMEM((k, tn), BF16)],
        compiler_params=_params(2),
        name="project",
    )(h, w_all)


def _ffn_up_kernel(h_ref, wg_ref, wu_ref, a_ref, wgb_ref, wub_ref):
    @pl.when(pl.program_id(1) == 0)
    def _():
        wgb_ref[...] = wg_ref[...].astype(BF16)
        wub_ref[...] = wu_ref[...].astype(BF16)
    h = h_ref[...]
    g = jnp.dot(h, wgb_ref[...], preferred_element_type=F32)
    u = jnp.dot(h, wub_ref[...], preferred_element_type=F32)
    a_ref[...] = (g * jax.nn.sigmoid(g) * u).astype(a_ref.dtype)


def ffn_up(h, wg_all, wu_all, layer):
    m, k = h.shape
    n = wg_all.shape[2]
    tm = min(m, 1024)
    tn = 512
    w_spec = pl.BlockSpec((None, k, tn), lambda j, i: (layer, 0, j))
    return pl.pallas_call(
        _ffn_up_kernel,
        grid=(n // tn, m // tm),
        in_specs=[pl.BlockSpec((tm, k), lambda j, i: (i, 0)), w_spec, w_spec],
        out_specs=pl.BlockSpec((tm, tn), lambda j, i: (i, j)),
        out_shape=jax.ShapeDtypeStruct((m, n), BF16),
        scratch_shapes=[pltpu.VMEM((k, tn), BF16), pltpu.VMEM((k, tn), BF16)],
        compiler_params=_params(2),
        name="ffn_up",
    )(h, wg_all, wu_all)


def _mm_res_kernel(*refs, k_sizes, scale):
    n_a = len(k_sizes)
    a_refs = refs[:n_a]
    w_ref, x_ref, o_ref, wb_ref = refs[n_a:]

    @pl.when(pl.program_id(1) == 0)
    def _():
        wb_ref[...] = w_ref[...].astype(BF16)

    acc = None
    off = 0
    for a_ref, ksz in zip(a_refs, k_sizes):
        d = jnp.dot(a_ref[...].astype(BF16), wb_ref[off:off + ksz, :],
                    preferred_element_type=F32)
        acc = d if acc is None else acc + d
        off += ksz
    o_ref[...] = x_ref[...] + scale * acc


def matmul_residual(a_list, w_all, layer, x, scale, tm, tn):
    m, n = x.shape
    k_sizes = tuple(a.shape[1] for a in a_list)
    k = sum(k_sizes)
    tm = min(m, tm)
    a_specs = [pl.BlockSpec((tm, ks), lambda j, i: (i, 0)) for ks in k_sizes]
    return pl.pallas_call(
        functools.partial(_mm_res_kernel, k_sizes=k_sizes, scale=scale),
        grid=(n // tn, m // tm),
        in_specs=a_specs + [pl.BlockSpec((None, k, tn), lambda j, i: (layer, 0, j)),
                            pl.BlockSpec((tm, tn), lambda j, i: (i, j))],
        out_specs=pl.BlockSpec((tm, tn), lambda j, i: (i, j)),
        out_shape=jax.ShapeDtypeStruct((m, n), F32),
        scratch_shapes=[pltpu.VMEM((k, tn), BF16)],
        compiler_params=_params(2),
        name="matmul_residual",
    )(*a_list, w_all, x)


def _softmax_pv(s, v):
    m = jnp.max(s, axis=-1, keepdims=True)
    p = jnp.exp(s - m)
    l = jnp.sum(p, axis=-1, keepdims=True)
    return jnp.dot(p.astype(BF16), v, preferred_element_type=F32), m, l


def _memory_attention_tile(qm_ref, mk_ref, mv_ref, om_ref):
    qm = (qm_ref[...] * SCALE).astype(BF16)
    s = _dot_t(qm, mk_ref[...].astype(BF16))
    o, _, l = _softmax_pv(s, mv_ref[...].astype(BF16))
    om_ref[...] = (o / l).astype(om_ref.dtype)


def _moba_prompt_kernel(slopes_ref, q_ref, k_ref, v_ref, qm_ref, mk_ref, mv_ref,
                        os_ref, om_ref, kb_ref, vt_ref, kmean_ref, bias_ref, nslope_ref,
                        chosen_ref):
    kvh = pl.program_id(1)
    i = pl.program_id(2)
    nb = kmean_ref.shape[0]
    tq = q_ref.shape[0]
    width = GROUP * tq
    key = lax.broadcasted_iota(jnp.int32, (MOBA_BLOCK, width), 0)
    query = lax.broadcasted_iota(jnp.int32, (MOBA_BLOCK, width), 1) & (tq - 1)

    @pl.when(i == 0)
    def _():
        kb_ref[...] = k_ref[...].astype(BF16)
        for n in range(nb):
            rows = slice(n * MOBA_BLOCK, (n + 1) * MOBA_BLOCK)
            vt_ref[n] = v_ref[rows, :].T.astype(BF16)
            kmean_ref[n:n + 1, :] = jnp.mean(k_ref[rows, :], axis=0, keepdims=True)
        lane = lax.broadcasted_iota(jnp.int32, (1, width), 1)
        nslope = jnp.zeros((1, width), F32)
        for g in range(GROUP):
            nslope = jnp.where((lane >= g * tq) & (lane < (g + 1) * tq),
                               -slopes_ref[kvh * GROUP + g], nslope)
        nslope_ref[...] = nslope
        bias_ref[...] = nslope * (query - key).astype(F32)

    q_t = jnp.concatenate(
        [q_ref[:, g * HEAD_DIM:(g + 1) * HEAD_DIM] for g in range(GROUP)], axis=0).T
    gate = jnp.dot(kmean_ref[...], q_t, precision=lax.Precision.HIGHEST,
                   preferred_element_type=F32)
    blk = lax.broadcasted_iota(jnp.int32, (nb, width), 0)
    rank = jnp.zeros((nb, width), jnp.int32)
    for m in range(nb - 1):
        gm = gate[m:m + 1, :]
        beats = (gm > gate) | ((gm == gate) & (m < blk))
        rank = rank + jnp.where(beats & (m < i), 1, 0)
    chosen_ref[...] = jnp.where((blk < i) & (rank < MOBA_TOPK), 1.0, 0.0)

    qb_t = (q_t * SCALE).astype(BF16)
    own_start = pl.multiple_of(i * MOBA_BLOCK, MOBA_BLOCK)
    s = jnp.dot(kb_ref[pl.ds(own_start, MOBA_BLOCK), :], qb_t,
                preferred_element_type=F32) + bias_ref[...]
    s = jnp.where(query >= key, s, MASKED)
    m_run = jnp.max(s, axis=0, keepdims=True)
    p = jnp.exp(s - m_run)
    l_run = jnp.sum(p, axis=0, keepdims=True)
    acc = jnp.dot(vt_ref[i], p.astype(BF16), preferred_element_type=F32)

    def past_block(n, carry):
        acc, m_run, l_run = carry
        start = pl.multiple_of(n * MOBA_BLOCK, MOBA_BLOCK)
        offset = ((i - n) * MOBA_BLOCK).astype(F32)
        shift = jnp.where(chosen_ref[pl.ds(n, 1), :] > 0.0, nslope_ref[...] * offset, MASKED)
        s = jnp.dot(kb_ref[pl.ds(start, MOBA_BLOCK), :], qb_t,
                    preferred_element_type=F32) + bias_ref[...] + shift
        m_new = jnp.maximum(m_run, jnp.max(s, axis=0, keepdims=True))
        alpha = jnp.exp(m_run - m_new)
        p = jnp.exp(s - m_new)
        l_new = alpha * l_run + jnp.sum(p, axis=0, keepdims=True)
        pv = jnp.dot(vt_ref[n], p.astype(BF16), preferred_element_type=F32)
        return alpha * acc + pv, m_new, l_new

    acc, m_run, l_run = lax.fori_loop(0, i, past_block, (acc, m_run, l_run))
    out_t = acc / l_run
    for g in range(GROUP):
        os_ref[:, g * HEAD_DIM:(g + 1) * HEAD_DIM] = (
            out_t[:, g * tq:(g + 1) * tq].T.astype(os_ref.dtype))

    _memory_attention_tile(qm_ref, mk_ref, mv_ref, om_ref)


def _swa_prompt_kernel(slopes_ref, sinks_ref, q_ref, k_ref, v_ref, qm_ref, mk_ref, mv_ref,
                       os_ref, om_ref, kb_ref, vt_ref, nslope_ref, sink_ref):
    kvh = pl.program_id(1)
    i = pl.program_id(2)
    tq = q_ref.shape[0]
    width = GROUP * tq
    n_chunks = tq // WINDOW + 1
    span = n_chunks * WINDOW

    @pl.when(i == 0)
    def _():
        kb_ref[...] = k_ref[...].astype(BF16)
        for n in range(vt_ref.shape[0]):
            vt_ref[n] = v_ref[n * WINDOW:(n + 1) * WINDOW, :].T.astype(BF16)
        lane = lax.broadcasted_iota(jnp.int32, (1, width), 1)
        nslope = jnp.zeros((1, width), F32)
        sink = jnp.zeros((1, width), F32)
        for g in range(GROUP):
            in_head = (lane >= g * tq) & (lane < (g + 1) * tq)
            nslope = jnp.where(in_head, -slopes_ref[kvh * GROUP + g], nslope)
            sink = jnp.where(in_head, sinks_ref[kvh * GROUP + g], sink)
        nslope_ref[...] = nslope
        sink_ref[...] = sink

    first_chunk = jnp.maximum(i * (tq // WINDOW) - 1, 0)
    kstart = pl.multiple_of(first_chunk * WINDOW, WINDOW)
    key = lax.broadcasted_iota(jnp.int32, (span, width), 0)
    query = lax.broadcasted_iota(jnp.int32, (span, width), 1) & (tq - 1)
    dist = query - key + (i * tq - kstart)
    visible = (dist >= 0) & (dist <= WINDOW)

    q_t = jnp.concatenate(
        [q_ref[:, g * HEAD_DIM:(g + 1) * HEAD_DIM] for g in range(GROUP)], axis=0).T
    qb_t = (q_t * SCALE).astype(BF16)
    s = jnp.dot(kb_ref[pl.ds(kstart, span), :], qb_t, preferred_element_type=F32)
    s = jnp.where(visible, s + nslope_ref[...] * dist.astype(F32), MASKED)
    sink = sink_ref[...]
    m = jnp.maximum(jnp.max(s, axis=0, keepdims=True), sink)
    p = jnp.exp(s - m)
    l = jnp.sum(p, axis=0, keepdims=True) + jnp.exp(sink - m)
    pb = p.astype(BF16)
    out_t = None
    for c in range(n_chunks):
        part = jnp.dot(vt_ref[first_chunk + c], pb[c * WINDOW:(c + 1) * WINDOW, :],
                       preferred_element_type=F32)
        out_t = part if out_t is None else out_t + part
    out_t = out_t / l
    for g in range(GROUP):
        os_ref[:, g * HEAD_DIM:(g + 1) * HEAD_DIM] = (
            out_t[:, g * tq:(g + 1) * tq].T.astype(os_ref.dtype))

    _memory_attention_tile(qm_ref, mk_ref, mv_ref, om_ref)


def prompt_attention(qkv, memkv, batch, seq, sinks_layer):
    nq = seq // Q_TILE
    k_col = Q_W // HEAD_DIM
    v_col = (Q_W + KV_W) // HEAD_DIM
    qm_col = (Q_W + 2 * KV_W) // HEAD_DIM
    smem = pl.BlockSpec(memory_space=pltpu.SMEM)
    tensor_specs = [
        pl.BlockSpec((Q_TILE, GROUP * HEAD_DIM), lambda b, h, i: (b * nq + i, h)),
        pl.BlockSpec((seq, HEAD_DIM), lambda b, h, i: (b, k_col + h)),
        pl.BlockSpec((seq, HEAD_DIM), lambda b, h, i: (b, v_col + h)),
        pl.BlockSpec((Q_TILE, HEAD_DIM), lambda b, h, i: (b * nq + i, qm_col + h)),
        pl.BlockSpec((N_MEM, HEAD_DIM), lambda b, h, i: (b, h)),
        pl.BlockSpec((N_MEM, HEAD_DIM), lambda b, h, i: (b, N_MEM_HEADS + h)),
    ]
    out_specs = [
        pl.BlockSpec((Q_TILE, GROUP * HEAD_DIM), lambda b, h, i: (b * nq + i, h)),
        pl.BlockSpec((Q_TILE, HEAD_DIM), lambda b, h, i: (b * nq + i, h)),
    ]
    out_shape = [jax.ShapeDtypeStruct((batch * seq, Q_W), BF16),
                 jax.ShapeDtypeStruct((batch * seq, MQ_W), BF16)]
    width = GROUP * Q_TILE
    row_scratch = pltpu.VMEM((1, width), F32)
    slopes = jnp.asarray(SLOPES)
    tensors = (qkv, qkv, qkv, qkv, memkv, memkv)
    if sinks_layer is None:
        return pl.pallas_call(
            _moba_prompt_kernel,
            grid=(batch, N_KV_HEADS, nq),
            in_specs=[smem] + tensor_specs,
            out_specs=out_specs, out_shape=out_shape,
            scratch_shapes=[
                pltpu.VMEM((seq, HEAD_DIM), BF16),
                pltpu.VMEM((seq // MOBA_BLOCK, HEAD_DIM, MOBA_BLOCK), BF16),
                pltpu.VMEM((seq // MOBA_BLOCK, HEAD_DIM), F32),
                pltpu.VMEM((MOBA_BLOCK, width), F32),
                row_scratch,
                pltpu.VMEM((seq // MOBA_BLOCK, width), F32),
            ],
            compiler_params=_params(3),
            name="moba_prompt",
        )(slopes, *tensors)
    return pl.pallas_call(
        _swa_prompt_kernel,
        grid=(batch, N_KV_HEADS, nq),
        in_specs=[smem, smem] + tensor_specs,
        out_specs=out_specs, out_shape=out_shape,
        scratch_shapes=[
            pltpu.VMEM((seq, HEAD_DIM), BF16),
            pltpu.VMEM((seq // WINDOW, HEAD_DIM, WINDOW), BF16),
            row_scratch,
            row_scratch,
        ],
        compiler_params=_params(3),
        name="swa_prompt",
    )(slopes, sinks_layer, *tensors)


def _head_rows(ref, head, n_tokens, n_heads):
    return ref[pl.ds(head, n_tokens, stride=n_heads), :]


def _row_scores(k, q_row):
    return jnp.sum(k * q_row, axis=-1, keepdims=True) * SCALE


def _sample_memory_kernel(qm_ref, mk_ref, mv_ref, o_ref):
    for h in range(N_MEM_HEADS):
        lanes = slice(h * HEAD_DIM, (h + 1) * HEAD_DIM)
        s = _row_scores(_head_rows(mk_ref, h, N_MEM, N_MEM_HEADS), qm_ref[:, lanes])
        p = jnp.exp(s - jnp.max(s, axis=0, keepdims=True))
        l = jnp.sum(p, axis=0, keepdims=True)
        v = _head_rows(mv_ref, h, N_MEM, N_MEM_HEADS)
        o_ref[:, lanes] = jnp.sum(p * v, axis=0, keepdims=True) / l


def sample_memory_attention(qm, mem_k, mem_v, layer):
    b = qm.shape[0]
    row = pl.BlockSpec((None, 1, MQ_W), lambda i: (i, 0, 0))
    cache = pl.BlockSpec((None, None, N_MEM * N_MEM_HEADS, HEAD_DIM), lambda i: (layer, i, 0, 0))
    return pl.pallas_call(
        _sample_memory_kernel,
        grid=(b,),
        in_specs=[row, cache, cache],
        out_specs=row,
        out_shape=jax.ShapeDtypeStruct((b, 1, MQ_W), F32),
        compiler_params=_params(1),
        name="sample_memory",
    )(qm, mem_k, mem_v)


def _sample_swa_kernel(sinks_ref, q_ref, kc_ref, vc_ref, kn_ref, vn_ref, o_ref):
    back = (WINDOW - lax.broadcasted_iota(jnp.int32, (WINDOW, 1), 0)).astype(F32)
    for hq in range(N_HEADS):
        kv = hq // GROUP
        q_row = q_ref[:, hq * HEAD_DIM:(hq + 1) * HEAD_DIM]
        lanes = slice(kv * HEAD_DIM, (kv + 1) * HEAD_DIM)
        sink = sinks_ref[hq]
        s_c = (_row_scores(_head_rows(kc_ref, kv, WINDOW, N_KV_HEADS), q_row)
               - float(SLOPES[hq]) * back)
        s_n = _row_scores(kn_ref[:, lanes], q_row)
        m = jnp.maximum(jnp.maximum(jnp.max(s_c, axis=0, keepdims=True), s_n), sink)
        p_c = jnp.exp(s_c - m)
        p_n = jnp.exp(s_n - m)
        l = jnp.sum(p_c, axis=0, keepdims=True) + p_n + jnp.exp(sink - m)
        v_c = _head_rows(vc_ref, kv, WINDOW, N_KV_HEADS)
        o = jnp.sum(p_c * v_c, axis=0, keepdims=True) + p_n * vn_ref[:, lanes]
        o_ref[:, hq * HEAD_DIM:(hq + 1) * HEAD_DIM] = o / l


def sample_swa_attention(q, k_new, v_new, cache_k, cache_v, sinks_layer, layer_b):
    b = q.shape[0]
    q_spec = pl.BlockSpec((None, 1, Q_W), lambda i: (i, 0, 0))
    kv_row = pl.BlockSpec((None, 1, KV_W), lambda i: (i, 0, 0))
    cache = pl.BlockSpec((None, None, WINDOW * N_KV_HEADS, HEAD_DIM),
                         lambda i: (layer_b, i, 0, 0))
    return pl.pallas_call(
        _sample_swa_kernel,
        grid=(b,),
        in_specs=[pl.BlockSpec(memory_space=pltpu.SMEM), q_spec, cache, cache, kv_row, kv_row],
        out_specs=q_spec,
        out_shape=jax.ShapeDtypeStruct((b, 1, Q_W), F32),
        compiler_params=_params(1),
        name="sample_swa",
    )(sinks_layer, q, cache_k, cache_v, k_new, v_new)


PAGE_ROWS = PAGE_SIZE * N_KV_HEADS
SUBLANES = 8


def _moba_gate_kernel(pt_ref, q_ref, *refs):
    del pt_ref
    pages = refs[:GATE_PAGES_PER_STEP]
    sel_ref, kmean_ref = refs[GATE_PAGES_PER_STEP:]
    s = pl.program_id(1)
    blocks_per_step = GATE_PAGES_PER_STEP // PAGES_PER_BLOCK
    n_blocks = kmean_ref.shape[1]

    sub = lax.broadcasted_iota(jnp.int32, (blocks_per_step, HEAD_DIM), 0)
    means = [jnp.zeros((blocks_per_step, HEAD_DIM), F32) for _ in range(N_KV_HEADS)]
    for r in range(blocks_per_step):
        total = None
        for page in pages[r * PAGES_PER_BLOCK:(r + 1) * PAGES_PER_BLOCK]:
            part = jnp.sum(page[...].reshape(PAGE_ROWS // SUBLANES, SUBLANES, HEAD_DIM), axis=0)
            total = part if total is None else total + part
        for kv in range(N_KV_HEADS):
            head_sum = total[kv:kv + 1, :] + total[kv + N_KV_HEADS:kv + N_KV_HEADS + 1, :]
            means[kv] = jnp.where(sub == r, head_sum * (1.0 / MOBA_BLOCK), means[kv])
    first = pl.multiple_of(s * blocks_per_step, blocks_per_step)
    for kv in range(N_KV_HEADS):
        kmean_ref[kv, pl.ds(first, blocks_per_step), :] = means[kv]

    @pl.when(s == pl.num_programs(1) - 1)
    def _():
        q = q_ref[...]
        head = lax.broadcasted_iota(jnp.int32, (N_HEADS, n_blocks), 0)
        gate = jnp.zeros((N_HEADS, n_blocks), F32)
        for kv in range(N_KV_HEADS):
            g_kv = _dot_t(q, kmean_ref[kv], precision=lax.Precision.HIGHEST)
            gate = jnp.where((head >= kv * GROUP) & (head < (kv + 1) * GROUP), g_kv, gate)
        blk = lax.broadcasted_iota(jnp.int32, (N_HEADS, n_blocks), 1).astype(F32)
        lane = lax.broadcasted_iota(jnp.int32, sel_ref.shape, 1)
        picks = jnp.zeros(sel_ref.shape, F32)
        for t in range(MOBA_TOPK):
            best = jnp.max(gate, axis=-1, keepdims=True)
            idx = jnp.min(jnp.where(gate == best, blk, float(n_blocks)), axis=-1, keepdims=True)
            picks = jnp.where(lane == t, idx, picks)
            gate = jnp.where(blk == idx, -jnp.inf, gate)
        sel_ref[...] = picks.astype(jnp.int32)


def sample_moba_select(q, cache_k, page_table, layer_a):
    b = q.shape[0]
    n_pages = page_table.shape[1]
    n_blocks = n_pages // PAGES_PER_BLOCK
    steps = n_pages // GATE_PAGES_PER_STEP

    def page_spec(r):
        return pl.BlockSpec(
            (None, None, PAGE_ROWS, HEAD_DIM),
            lambda i, s, pt: (layer_a, pt[i, s * GATE_PAGES_PER_STEP + r], 0, 0))

    grid_spec = pltpu.PrefetchScalarGridSpec(
        num_scalar_prefetch=1,
        grid=(b, steps),
        in_specs=[pl.BlockSpec((None, N_HEADS, HEAD_DIM), lambda i, s, pt: (i, 0, 0))]
        + [page_spec(r) for r in range(GATE_PAGES_PER_STEP)],
        out_specs=pl.BlockSpec((None, N_HEADS, HEAD_DIM), lambda i, s, pt: (i, 0, 0)),
        scratch_shapes=[pltpu.VMEM((N_KV_HEADS, n_blocks, HEAD_DIM), F32)],
    )
    sel = pl.pallas_call(
        _moba_gate_kernel,
        grid_spec=grid_spec,
        out_shape=jax.ShapeDtypeStruct((b, N_HEADS, HEAD_DIM), jnp.int32),
        compiler_params=_params(2),
        name="sample_moba_select",
    )(page_table, q, *([cache_k] * GATE_PAGES_PER_STEP))
    return sel[:, :, :MOBA_TOPK]


N_SEL_PAGES = GROUP * MOBA_TOPK * PAGES_PER_BLOCK


def _moba_sample_kernel(pt_ref, sel_ref, slopes_ref, q_ref, kn_ref, vn_ref, *refs, past_len):
    del pt_ref
    k_pages = refs[:N_SEL_PAGES]
    v_pages = refs[N_SEL_PAGES:2 * N_SEL_PAGES]
    o_ref = refs[2 * N_SEL_PAGES]
    b = pl.program_id(0)
    kvh = pl.program_id(1)
    within = lax.broadcasted_iota(jnp.int32, (PAGE_SIZE, 1), 0)
    for g in range(GROUP):
        head = kvh * GROUP + g
        slope = slopes_ref[head]
        q_row = q_ref[g:g + 1, :]
        s_own = _row_scores(kn_ref[...], q_row)
        scores = []
        m = s_own
        for t in range(MOBA_TOPK):
            first = sel_ref[(b * N_HEADS + head) * MOBA_TOPK + t] * MOBA_BLOCK
            for r in range(PAGES_PER_BLOCK):
                page = k_pages[(g * MOBA_TOPK + t) * PAGES_PER_BLOCK + r]
                dist = (past_len - (first + r * PAGE_SIZE) - within).astype(F32)
                s = _row_scores(_head_rows(page, kvh, PAGE_SIZE, N_KV_HEADS), q_row) - slope * dist
                scores.append(s)
                m = jnp.maximum(m, jnp.max(s, axis=0, keepdims=True))
        p_own = jnp.exp(s_own - m)
        l = p_own
        o = p_own * vn_ref[...]
        for idx, s in enumerate(scores):
            p = jnp.exp(s - m)
            l = l + jnp.sum(p, axis=0, keepdims=True)
            v = _head_rows(v_pages[g * MOBA_TOPK * PAGES_PER_BLOCK + idx], kvh, PAGE_SIZE, N_KV_HEADS)
            o = o + jnp.sum(p * v, axis=0, keepdims=True)
        o_ref[g:g + 1, :] = o / l


def sample_moba_attention(q, k_new, v_new, cache_k, cache_v, page_table, sel, layer_a):
    b = q.shape[0]
    past_len = page_table.shape[1] * PAGE_SIZE

    def page_spec(g, t, r):
        def index_map(i, h, pt, sl):
            block = sl[(i * N_HEADS + h * GROUP + g) * MOBA_TOPK + t]
            return (layer_a, pt[i, block * PAGES_PER_BLOCK + r], 0, 0)
        return pl.BlockSpec((None, None, PAGE_ROWS, HEAD_DIM), index_map)

    page_specs = [page_spec(g, t, r) for g in range(GROUP) for t in range(MOBA_TOPK)
                  for r in range(PAGES_PER_BLOCK)]
    q_spec = pl.BlockSpec((None, None, GROUP, HEAD_DIM), lambda i, h, pt, sl: (i, h, 0, 0))
    row_spec = pl.BlockSpec((None, None, 1, HEAD_DIM), lambda i, h, pt, sl: (i, h, 0, 0))
    grid_spec = pltpu.PrefetchScalarGridSpec(
        num_scalar_prefetch=2,
        grid=(b, N_KV_HEADS),
        in_specs=[pl.BlockSpec(memory_space=pltpu.SMEM), q_spec, row_spec, row_spec]
        + page_specs + page_specs,
        out_specs=q_spec,
    )
    return pl.pallas_call(
        functools.partial(_moba_sample_kernel, past_len=past_len),
        grid_spec=grid_spec,
        out_shape=jax.ShapeDtypeStruct((b, N_KV_HEADS, GROUP, HEAD_DIM), F32),
        compiler_params=_params(2),
        name="sample_moba",
    )(page_table, sel.reshape(-1), jnp.asarray(SLOPES), q, k_new, v_new,
      *([cache_k] * N_SEL_PAGES), *([cache_v] * N_SEL_PAGES))


def _half_ffn(x, g_all, wg_all, wu_all, wd_all, layer):
    h = rms_norm(x, g_all, layer, BF16)
    a = ffn_up(h, wg_all, wu_all, layer)
    return matmul_residual([a], wd_all, layer, x, 0.5, tm=512, tn=512)


def kernel(x_prompt, x_sample, cache_moba_k, cache_moba_v, cache_swa_k, cache_swa_v, cache_mem_k, cache_mem_v, page_table, mem_prompt, g_ffn1, w_ffn1_gate, w_ffn1_up, w_ffn1_down, g_attn, w_in, w_out, sinks, g_mem, w_mem_kv, g_ffn2, w_ffn2_gate, w_ffn2_up, w_ffn2_down, g_final):
    bp, seq, d = x_prompt.shape
    bs = x_sample.shape[0]
    n_pool = cache_moba_k.shape[1]
    xp = x_prompt.reshape(bp * seq, d)
    xs = x_sample.reshape(bs, d)
    mem = mem_prompt.reshape(bp * N_MEM, d)
    mem_k_cache = cache_mem_k.reshape(DEPTH, bs, N_MEM * N_MEM_HEADS, HEAD_DIM)
    mem_v_cache = cache_mem_v.reshape(DEPTH, bs, N_MEM * N_MEM_HEADS, HEAD_DIM)
    swa_k_cache = cache_swa_k.reshape(-1, bs, WINDOW * N_KV_HEADS, HEAD_DIM)
    swa_v_cache = cache_swa_v.reshape(-1, bs, WINDOW * N_KV_HEADS, HEAD_DIM)
    moba_k_flat = cache_moba_k.reshape(-1, n_pool, PAGE_ROWS, HEAD_DIM)
    moba_v_flat = cache_moba_v.reshape(-1, n_pool, PAGE_ROWS, HEAD_DIM)

    moba_kp, moba_vp, moba_ks, moba_vs = [], [], [], []
    swa_kp, swa_vp, swa_ks, swa_vs = [], [], [], []
    mem_kp, mem_vp = [], []
    for l in range(DEPTH):
        xp = _half_ffn(xp, g_ffn1, w_ffn1_gate, w_ffn1_up, w_ffn1_down, l)
        xs = _half_ffn(xs, g_ffn1, w_ffn1_gate, w_ffn1_up, w_ffn1_down, l)

        qkv_p = project(rms_norm(xp, g_attn, l, BF16), w_in, l, tn=1024)
        qkv_s = project(rms_norm(xs, g_attn, l, BF16), w_in, l, tn=1024)
        memkv = project(rms_norm(mem, g_mem, l, BF16), w_mem_kv, l, tn=512)
        mem_kp.append(memkv[:, :MQ_W].reshape(bp, N_MEM, N_MEM_HEADS, HEAD_DIM))
        mem_vp.append(memkv[:, MQ_W:].reshape(bp, N_MEM, N_MEM_HEADS, HEAD_DIM))

        kp = qkv_p[:, Q_W:Q_W + KV_W].reshape(bp, seq, N_KV_HEADS, HEAD_DIM)
        vp = qkv_p[:, Q_W + KV_W:Q_W + 2 * KV_W].reshape(bp, seq, N_KV_HEADS, HEAD_DIM)
        qs = qkv_s[:, :Q_W]
        ks = qkv_s[:, Q_W:Q_W + KV_W]
        vs = qkv_s[:, Q_W + KV_W:Q_W + 2 * KV_W]
        qms = qkv_s[:, Q_W + 2 * KV_W:]
        ks4 = ks.reshape(bs, 1, N_KV_HEADS, HEAD_DIM)
        vs4 = vs.reshape(bs, 1, N_KV_HEADS, HEAD_DIM)

        oms = sample_memory_attention(qms.reshape(bs, 1, MQ_W), mem_k_cache, mem_v_cache, l)
        j = l // 2
        if l % 2 == 0:
            osp, omp = prompt_attention(qkv_p, memkv, bp, seq, None)
            sel = sample_moba_select(qs.reshape(bs, N_HEADS, HEAD_DIM), moba_k_flat, page_table, j)
            oss = sample_moba_attention(
                qs.reshape(bs, N_KV_HEADS, GROUP, HEAD_DIM),
                ks.reshape(bs, N_KV_HEADS, 1, HEAD_DIM), vs.reshape(bs, N_KV_HEADS, 1, HEAD_DIM),
                moba_k_flat, moba_v_flat, page_table, sel, j)
            moba_kp.append(kp)
            moba_vp.append(vp)
            moba_ks.append(ks4)
            moba_vs.append(vs4)
        else:
            osp, omp = prompt_attention(qkv_p, memkv, bp, seq, sinks[j])
            oss = sample_swa_attention(qs.reshape(bs, 1, Q_W), ks.reshape(bs, 1, KV_W),
                                       vs.reshape(bs, 1, KV_W), swa_k_cache, swa_v_cache,
                                       sinks[j], j)
            swa_kp.append(kp[:, -WINDOW:])
            swa_vp.append(vp[:, -WINDOW:])
            swa_ks.append(jnp.concatenate([cache_swa_k[j], ks4], axis=1)[:, -WINDOW:])
            swa_vs.append(jnp.concatenate([cache_swa_v[j], vs4], axis=1)[:, -WINDOW:])

        xp = matmul_residual([osp, omp], w_out, l, xp, 1.0, tm=1024, tn=1024)
        xs = matmul_residual([oss.reshape(bs, Q_W), oms.reshape(bs, MQ_W)], w_out, l, xs, 1.0,
                             tm=1024, tn=1024)

        xp = _half_ffn(xp, g_ffn2, w_ffn2_gate, w_ffn2_up, w_ffn2_down, l)
        xs = _half_ffn(xs, g_ffn2, w_ffn2_gate, w_ffn2_up, w_ffn2_down, l)

    g_fin = g_final.reshape(1, d)
    y_prompt = rms_norm(xp, g_fin, 0, F32).reshape(bp, seq, d)
    y_sample = rms_norm(xs, g_fin, 0, F32).reshape(bs, 1, d)
    return (y_prompt, y_sample,
            jnp.stack(moba_kp), jnp.stack(moba_vp), jnp.stack(moba_ks), jnp.stack(moba_vs),
            jnp.stack(swa_kp), jnp.stack(swa_vp), jnp.stack(swa_ks), jnp.stack(swa_vs),
            jnp.stack(mem_kp), jnp.stack(mem_vp))
```

```python
import functools
import math

import jax
import jax.numpy as jnp
import numpy as np
from jax import lax
from jax.experimental import pallas as pl
from jax.experimental.pallas import tpu as pltpu

F32 = jnp.float32
BF16 = jnp.bfloat16

D_MODEL = 2048
DEPTH = 4
PAGE_SIZE = 128
HEAD_DIM = 128
N_HEADS = 12
N_KV_HEADS = 4
GROUP = N_HEADS // N_KV_HEADS
N_MEM_HEADS = 4
N_MEM = 256
D_FF = 5632
MOBA_BLOCK = 256
MOBA_TOPK = 3
WINDOW = 128
RMS_EPS = 1e-6
Q_W = N_HEADS * HEAD_DIM
KV_W = N_KV_HEADS * HEAD_DIM
MQ_W = N_MEM_HEADS * HEAD_DIM
IN_W = Q_W + 2 * KV_W + MQ_W
SCALE = HEAD_DIM ** -0.5
PAGES_PER_BLOCK = MOBA_BLOCK // PAGE_SIZE

VMEM_LIMIT_BYTES = 56 * 1024 * 1024
MASKED = -1e30
Q_TILE = 256
GATE_PAGES_PER_STEP = 16


def _alibi_slope_list(n):
    def pow2(m):
        start = 2.0 ** (-(2.0 ** -(math.log2(m) - 3)))
        return [start ** (i + 1) for i in range(m)]
    if math.log2(n).is_integer():
        return pow2(n)
    c = 2 ** math.floor(math.log2(n))
    return pow2(c) + _alibi_slope_list(2 * c)[0::2][:n - c]


SLOPES = np.array(_alibi_slope_list(N_HEADS), dtype=np.float32)


def _params(n_grid_dims):
    return pltpu.CompilerParams(
        dimension_semantics=("arbitrary",) * n_grid_dims,
        vmem_limit_bytes=VMEM_LIMIT_BYTES)


def _dot_t(a, b, precision=None):
    return lax.dot_general(a, b, (((1,), (1,)), ((), ())), precision=precision,
                           preferred_element_type=F32)


def _norm_kernel(x_ref, g_ref, o_ref):
    x = x_ref[...]
    ms = jnp.mean(x * x, axis=-1, keepdims=True)
    o_ref[...] = (x * lax.rsqrt(ms + RMS_EPS) * g_ref[...]).astype(o_ref.dtype)


def rms_norm(x, g_all, layer, out_dtype):
    m, d = x.shape
    tm = min(m, 512)
    g3 = g_all.reshape(g_all.shape[0], 1, d)
    return pl.pallas_call(
        _norm_kernel,
        grid=(m // tm,),
        in_specs=[pl.BlockSpec((tm, d), lambda i: (i, 0)),
                  pl.BlockSpec((None, 1, d), lambda i: (layer, 0, 0))],
        out_specs=pl.BlockSpec((tm, d), lambda i: (i, 0)),
        out_shape=jax.ShapeDtypeStruct((m, d), out_dtype),
        compiler_params=_params(1),
        name="rms_norm",
    )(x, g3)


def _proj_kernel(h_ref, w_ref, o_ref, wb_ref):
    @pl.when(pl.program_id(1) == 0)
    def _():
        wb_ref[...] = w_ref[...].astype(BF16)
    o_ref[...] = jnp.dot(h_ref[...], wb_ref[...], preferred_element_type=F32)


def project(h, w_all, layer, tn):
    m, k = h.shape
    n = w_all.shape[2]
    tm = min(m, 1024)
    return pl.pallas_call(
        _proj_kernel,
        grid=(n // tn, m // tm),
        in_specs=[pl.BlockSpec((tm, k), lambda j, i: (i, 0)),
                  pl.BlockSpec((None, k, tn), lambda j, i: (layer, 0, j))],
        out_specs=pl.BlockSpec((tm, tn), lambda j, i: (i, j)),
        out_shape=jax.ShapeDtypeStruct((m, n), F32),
        scratch_shapes=[pltpu.VMEM((k, tn), BF16)],
        compiler_params=_params(2),
        name="project",
    )(h, w_all)


LANES = 128


def _for_row_tiles(i, n_full, tail, body):
    @pl.when(i < n_full)
    def _():
        body(slice(None))

    @pl.when(i == n_full)
    def _():
        body(slice(0, tail))


def _lane_partial_sums(x):
    total = x[:, :LANES]
    for c in range(1, x.shape[1] // LANES):
        total = total + x[:, c * LANES:(c + 1) * LANES]
    return total


def _store_norm_inputs(x, g_ref, xg_ref, ssq_ref, rows):
    xg_ref[rows, :] = (x * g_ref[...]).astype(xg_ref.dtype)
    ssq_ref[rows, :] = _lane_partial_sums(x * x)


def _inv_rms(ssq, d_model):
    total = jnp.sum(jnp.sum(ssq, axis=0), axis=-1, keepdims=True)
    return lax.rsqrt(total * (1.0 / d_model) + RMS_EPS)


def _stream_start_kernel(xp_ref, xs_ref, g_ref, x_ref, xg_ref, ssq_ref, *, n_full, tail):
    def emit(src_ref):
        def body(rows):
            x = src_ref[rows, :]
            x_ref[rows, :] = x
            _store_norm_inputs(x, g_ref, xg_ref, ssq_ref, rows)
        return body
    i = pl.program_id(0)

    @pl.when(i < n_full)
    def _():
        emit(xp_ref)(slice(None))

    @pl.when(i == n_full)
    def _():
        emit(xs_ref)(slice(0, tail))


def stream_start(x_main, x_tail, g_all, layer, tm):
    m, d = x_main.shape
    tail = x_tail.shape[0]
    n_full = m // tm
    rows = m + tail
    g3 = g_all.reshape(g_all.shape[0], 1, d)
    row_block = lambda i: (i, 0)
    return pl.pallas_call(
        functools.partial(_stream_start_kernel, n_full=n_full, tail=tail),
        grid=(n_full + 1,),
        in_specs=[pl.BlockSpec((tm, d), lambda i: (jnp.minimum(i, n_full - 1), 0)),
                  pl.BlockSpec((tail, d), lambda i: (0, 0)),
                  pl.BlockSpec((None, 1, d), lambda i: (layer, 0, 0))],
        out_specs=[pl.BlockSpec((tm, d), row_block), pl.BlockSpec((tm, d), row_block),
                   pl.BlockSpec((None, tm, LANES), lambda i: (0, i, 0))],
        out_shape=[jax.ShapeDtypeStruct((rows, d), F32), jax.ShapeDtypeStruct((rows, d), BF16),
                   jax.ShapeDtypeStruct((1, rows, LANES), F32)],
        compiler_params=_params(1),
        name="stream_start",
    )(x_main, x_tail, g3)


def _stream_end_kernel(x_ref, g_ref, yp_ref, ys_ref, *, n_full, tail):
    def normed(rows):
        x = x_ref[rows, :]
        ms = jnp.mean(x * x, axis=-1, keepdims=True)
        return x * lax.rsqrt(ms + RMS_EPS) * g_ref[...]
    i = pl.program_id(0)

    @pl.when(i < n_full)
    def _():
        yp_ref[...] = normed(slice(None))

    @pl.when(i == n_full)
    def _():
        ys_ref[...] = normed(slice(0, tail))


def stream_end(x, g, m, tm):
    rows, d = x.shape
    tail = rows - m
    n_full = m // tm
    return pl.pallas_call(
        functools.partial(_stream_end_kernel, n_full=n_full, tail=tail),
        grid=(n_full + 1,),
        in_specs=[pl.BlockSpec((tm, d), lambda i: (i, 0)),
                  pl.BlockSpec((1, d), lambda i: (0, 0))],
        out_specs=[pl.BlockSpec((tm, d), lambda i: (jnp.minimum(i, n_full - 1), 0)),
                   pl.BlockSpec((tail, d), lambda i: (0, 0))],
        out_shape=[jax.ShapeDtypeStruct((m, d), F32), jax.ShapeDtypeStruct((tail, d), F32)],
        compiler_params=_params(1),
        name="stream_end",
    )(x, g.reshape(1, d))


def _normed_specs(nj, tm, k):
    return [pl.BlockSpec((tm, k), lambda j, i: (i, 0)),
            pl.BlockSpec((nj, tm, LANES), lambda j, i: (0, jnp.where(j == 0, i, 0), 0))]


def _cached_inv_rms(ssq_ref, inv_ref, j, i, rows, d_model):
    @pl.when(j == 0)
    def _():
        inv_ref[i, rows, :] = _inv_rms(ssq_ref[:, rows, :], d_model)
    return inv_ref[i, rows, :]


def _norm_proj_kernel(xg_ref, ssq_ref, w_ref, o_ref, wb_ref, inv_ref, *, n_full, tail):
    j = pl.program_id(0)
    i = pl.program_id(1)

    @pl.when(i == 0)
    def _():
        wb_ref[...] = w_ref[...].astype(BF16)

    def body(rows):
        inv = _cached_inv_rms(ssq_ref, inv_ref, j, i, rows, xg_ref.shape[1])
        o_ref[rows, :] = inv * jnp.dot(xg_ref[rows, :], wb_ref[...], preferred_element_type=F32)
    _for_row_tiles(i, n_full, tail, body)


def norm_project(xg, ssq, w_all, layer, m, tm, tn):
    rows, k = xg.shape
    n = w_all.shape[2]
    n_full = m // tm
    return pl.pallas_call(
        functools.partial(_norm_proj_kernel, n_full=n_full, tail=rows - m),
        grid=(n // tn, n_full + 1),
        in_specs=_normed_specs(ssq.shape[0], tm, k)
        + [pl.BlockSpec((None, k, tn), lambda j, i: (layer, 0, j))],
        out_specs=pl.BlockSpec((tm, tn), lambda j, i: (i, j)),
        out_shape=jax.ShapeDtypeStruct((rows, n), F32),
        scratch_shapes=[pltpu.VMEM((k, tn), BF16), pltpu.VMEM((n_full + 1, tm, 1), F32)],
        compiler_params=_params(2),
        name="norm_project",
    )(xg, ssq, w_all)


def _ffn_up_kernel(xg_ref, ssq_ref, wg_ref, wu_ref, a_ref, wgb_ref, wub_ref, inv_ref,
                   *, n_full, tail):
    j = pl.program_id(0)
    i = pl.program_id(1)

    @pl.when(i == 0)
    def _():
        wgb_ref[...] = wg_ref[...].astype(BF16)
        wub_ref[...] = wu_ref[...].astype(BF16)

    def body(rows):
        inv = _cached_inv_rms(ssq_ref, inv_ref, j, i, rows, xg_ref.shape[1])
        h = xg_ref[rows, :]
        g = inv * jnp.dot(h, wgb_ref[...], preferred_element_type=F32)
        u = inv * jnp.dot(h, wub_ref[...], preferred_element_type=F32)
        a_ref[rows, :] = (g * jax.nn.sigmoid(g) * u).astype(a_ref.dtype)
    _for_row_tiles(i, n_full, tail, body)


def ffn_up(xg, ssq, wg_all, wu_all, layer, m, tm, tn):
    rows, k = xg.shape
    n = wg_all.shape[2]
    n_full = m // tm
    w_spec = pl.BlockSpec((None, k, tn), lambda j, i: (layer, 0, j))
    return pl.pallas_call(
        functools.partial(_ffn_up_kernel, n_full=n_full, tail=rows - m),
        grid=(n // tn, n_full + 1),
        in_specs=_normed_specs(ssq.shape[0], tm, k) + [w_spec, w_spec],
        out_specs=pl.BlockSpec((tm, tn), lambda j, i: (i, j)),
        out_shape=jax.ShapeDtypeStruct((rows, n), BF16),
        scratch_shapes=[pltpu.VMEM((k, tn), BF16), pltpu.VMEM((k, tn), BF16),
                        pltpu.VMEM((n_full + 1, tm, 1), F32)],
        compiler_params=_params(2),
        name="ffn_up",
    )(xg, ssq, wg_all, wu_all)


def _mm_res_kernel(*refs, k_sizes, split_tail, emit_norm_inputs, scale, n_full, tail):
    n_a = len(k_sizes)
    main_refs = refs[:n_a]
    tail_refs = refs[n_a:2 * n_a] if split_tail else main_refs
    rest = refs[2 * n_a:] if split_tail else refs[n_a:]
    if emit_norm_inputs:
        w_ref, x_ref, g_ref, o_ref, xg_ref, ssq_ref, wb_ref = rest
    else:
        w_ref, x_ref, o_ref, wb_ref = rest
    i = pl.program_id(1)

    @pl.when(i == 0)
    def _():
        wb_ref[...] = w_ref[...].astype(BF16)

    def body(rows):
        a_refs = main_refs if rows == slice(None) else tail_refs
        acc = None
        off = 0
        for a_ref, ksz in zip(a_refs, k_sizes):
            d = jnp.dot(a_ref[rows, :].astype(BF16), wb_ref[off:off + ksz, :],
                        preferred_element_type=F32)
            acc = d if acc is None else acc + d
            off += ksz
        x = x_ref[rows, :] + scale * acc
        o_ref[rows, :] = x
        if emit_norm_inputs:
            _store_norm_inputs(x, g_ref, xg_ref, ssq_ref, rows)
    _for_row_tiles(i, n_full, tail, body)


def matmul_residual(a_main, a_tail, w_all, layer, x, scale, m, tm, tn, next_gain=None):
    rows, n = x.shape
    tail = rows - m
    n_full = m // tm
    k_sizes = tuple(a.shape[1] for a in a_main)
    k = sum(k_sizes)
    nj = n // tn
    split_tail = a_tail is not None
    if split_tail:
        a_specs = [pl.BlockSpec((tm, ks), lambda j, i: (jnp.minimum(i, n_full - 1), 0))
                   for ks in k_sizes]
        a_specs += [pl.BlockSpec((tail, ks), lambda j, i: (0, 0)) for ks in k_sizes]
        operands = list(a_main) + list(a_tail)
    else:
        a_specs = [pl.BlockSpec((tm, ks), lambda j, i: (i, 0)) for ks in k_sizes]
        operands = list(a_main)
    tile = pl.BlockSpec((tm, tn), lambda j, i: (i, j))
    in_specs = a_specs + [pl.BlockSpec((None, k, tn), lambda j, i: (layer, 0, j)), tile]
    operands += [w_all, x]
    out_specs = [tile]
    out_shape = [jax.ShapeDtypeStruct((rows, n), F32)]
    emit = next_gain is not None
    if emit:
        g_all, g_layer = next_gain
        in_specs.append(pl.BlockSpec((None, 1, tn), lambda j, i: (g_layer, 0, j)))
        operands.append(g_all.reshape(-1, 1, n))
        out_specs += [tile, pl.BlockSpec((None, tm, LANES), lambda j, i: (j, i, 0))]
        out_shape += [jax.ShapeDtypeStruct((rows, n), BF16),
                      jax.ShapeDtypeStruct((nj, rows, LANES), F32)]
    return pl.pallas_call(
        functools.partial(_mm_res_kernel, k_sizes=k_sizes, split_tail=split_tail,
                          emit_norm_inputs=emit, scale=scale, n_full=n_full, tail=tail),
        grid=(nj, n_full + 1),
        in_specs=in_specs, out_specs=out_specs, out_shape=out_shape,
        scratch_shapes=[pltpu.VMEM((k, tn), BF16)],
        compiler_params=_params(2),
        name="matmul_residual",
    )(*operands)


def _softmax_pv(s, v):
    m = jnp.max(s, axis=-1, keepdims=True)
    p = jnp.exp(s - m)
    l = jnp.sum(p, axis=-1, keepdims=True)
    return jnp.dot(p.astype(BF16), v, preferred_element_type=F32), m, l


def _memory_attention_tile(qm_ref, mk_ref, mv_ref, om_ref):
    qm = (qm_ref[...] * SCALE).astype(BF16)
    s = _dot_t(qm, mk_ref[...].astype(BF16))
    o, _, l = _softmax_pv(s, mv_ref[...].astype(BF16))
    om_ref[...] = (o / l).astype(om_ref.dtype)


def _moba_prompt_kernel(slopes_ref, q_ref, k_ref, v_ref, qm_ref, mk_ref, mv_ref,
                        os_ref, om_ref, kb_ref, vt_ref, kmean_ref, bias_ref, nslope_ref,
                        chosen_ref):
    kvh = pl.program_id(1)
    i = pl.program_id(2)
    nb = kmean_ref.shape[0]
    tq = q_ref.shape[0]
    width = GROUP * tq
    key = lax.broadcasted_iota(jnp.int32, (MOBA_BLOCK, width), 0)
    query = lax.broadcasted_iota(jnp.int32, (MOBA_BLOCK, width), 1) & (tq - 1)

    @pl.when(i == 0)
    def _():
        kb_ref[...] = k_ref[...].astype(BF16)
        for n in range(nb):
            rows = slice(n * MOBA_BLOCK, (n + 1) * MOBA_BLOCK)
            vt_ref[n] = v_ref[rows, :].T.astype(BF16)
            kmean_ref[n:n + 1, :] = jnp.mean(k_ref[rows, :], axis=0, keepdims=True)
        lane = lax.broadcasted_iota(jnp.int32, (1, width), 1)
        nslope = jnp.zeros((1, width), F32)
        for g in range(GROUP):
            nslope = jnp.where((lane >= g * tq) & (lane < (g + 1) * tq),
                               -slopes_ref[kvh * GROUP + g], nslope)
        nslope_ref[...] = nslope
        bias_ref[...] = nslope * (query - key).astype(F32)

    q_t = jnp.concatenate(
        [q_ref[:, g * HEAD_DIM:(g + 1) * HEAD_DIM] for g in range(GROUP)], axis=0).T
    gate = jnp.dot(kmean_ref[...], q_t, precision=lax.Precision.HIGHEST,
                   preferred_element_type=F32)
    blk = lax.broadcasted_iota(jnp.int32, (nb, width), 0)
    rank = jnp.zeros((nb, width), jnp.int32)
    for m in range(nb - 1):
        gm = gate[m:m + 1, :]
        beats = (gm > gate) | ((gm == gate) & (m < blk))
        rank = rank + jnp.where(beats & (m < i), 1, 0)
    chosen_ref[...] = jnp.where((blk < i) & (rank < MOBA_TOPK), 1.0, 0.0)

    qb_t = (q_t * SCALE).astype(BF16)
    own_start = pl.multiple_of(i * MOBA_BLOCK, MOBA_BLOCK)
    s = jnp.dot(kb_ref[pl.ds(own_start, MOBA_BLOCK), :], qb_t,
                preferred_element_type=F32) + bias_ref[...]
    s = jnp.where(query >= key, s, MASKED)
    m_run = jnp.max(s, axis=0, keepdims=True)
    p = jnp.exp(s - m_run)
    l_run = jnp.sum(p, axis=0, keepdims=True)
    acc = jnp.dot(vt_ref[i], p.astype(BF16), preferred_element_type=F32)

    def past_block(n, carry):
        acc, m_run, l_run = carry
        start = pl.multiple_of(n * MOBA_BLOCK, MOBA_BLOCK)
        offset = ((i - n) * MOBA_BLOCK).astype(F32)
        shift = jnp.where(chosen_ref[pl.ds(n, 1), :] > 0.0, nslope_ref[...] * offset, MASKED)
        s = jnp.dot(kb_ref[pl.ds(start, MOBA_BLOCK), :], qb_t,
                    preferred_element_type=F32) + bias_ref[...] + shift
        m_new = jnp.maximum(m_run, jnp.max(s, axis=0, keepdims=True))
        alpha = jnp.exp(m_run - m_new)
        p = jnp.exp(s - m_new)
        l_new = alpha * l_run + jnp.sum(p, axis=0, keepdims=True)
        pv = jnp.dot(vt_ref[n], p.astype(BF16), preferred_element_type=F32)
        return alpha * acc + pv, m_new, l_new

    acc, m_run, l_run = lax.fori_loop(0, i, past_block, (acc, m_run, l_run))
    out_t = acc / l_run
    for g in range(GROUP):
        os_ref[:, g * HEAD_DIM:(g + 1) * HEAD_DIM] = (
            out_t[:, g * tq:(g + 1) * tq].T.astype(os_ref.dtype))

    _memory_attention_tile(qm_ref, mk_ref, mv_ref, om_ref)


def _swa_prompt_kernel(slopes_ref, sinks_ref, q_ref, k_ref, v_ref, qm_ref, mk_ref, mv_ref,
                       os_ref, om_ref, kb_ref, vt_ref, nslope_ref, sink_ref):
    kvh = pl.program_id(1)
    i = pl.program_id(2)
    tq = q_ref.shape[0]
    width = GROUP * tq
    n_chunks = tq // WINDOW + 1
    span = n_chunks * WINDOW

    @pl.when(i == 0)
    def _():
        kb_ref[...] = k_ref[...].astype(BF16)
        for n in range(vt_ref.shape[0]):
            vt_ref[n] = v_ref[n * WINDOW:(n + 1) * WINDOW, :].T.astype(BF16)
        lane = lax.broadcasted_iota(jnp.int32, (1, width), 1)
        nslope = jnp.zeros((1, width), F32)
        sink = jnp.zeros((1, width), F32)
        for g in range(GROUP):
            in_head = (lane >= g * tq) & (lane < (g + 1) * tq)
            nslope = jnp.where(in_head, -slopes_ref[kvh * GROUP + g], nslope)
            sink = jnp.where(in_head, sinks_ref[kvh * GROUP + g], sink)
        nslope_ref[...] = nslope
        sink_ref[...] = sink

    first_chunk = jnp.maximum(i * (tq // WINDOW) - 1, 0)
    kstart = pl.multiple_of(first_chunk * WINDOW, WINDOW)
    key = lax.broadcasted_iota(jnp.int32, (span, width), 0)
    query = lax.broadcasted_iota(jnp.int32, (span, width), 1) & (tq - 1)
    dist = query - key + (i * tq - kstart)
    visible = (dist >= 0) & (dist <= WINDOW)

    q_t = jnp.concatenate(
        [q_ref[:, g * HEAD_DIM:(g + 1) * HEAD_DIM] for g in range(GROUP)], axis=0).T
    qb_t = (q_t * SCALE).astype(BF16)
    s = jnp.dot(kb_ref[pl.ds(kstart, span), :], qb_t, preferred_element_type=F32)
    s = jnp.where(visible, s + nslope_ref[...] * dist.astype(F32), MASKED)
    sink = sink_ref[...]
    m = jnp.maximum(jnp.max(s, axis=0, keepdims=True), sink)
    p = jnp.exp(s - m)
    l = jnp.sum(p, axis=0, keepdims=True) + jnp.exp(sink - m)
    pb = p.astype(BF16)
    out_t = None
    for c in range(n_chunks):
        part = jnp.dot(vt_ref[first_chunk + c], pb[c * WINDOW:(c + 1) * WINDOW, :],
                       preferred_element_type=F32)
        out_t = part if out_t is None else out_t + part
    out_t = out_t / l
    for g in range(GROUP):
        os_ref[:, g * HEAD_DIM:(g + 1) * HEAD_DIM] = (
            out_t[:, g * tq:(g + 1) * tq].T.astype(os_ref.dtype))

    _memory_attention_tile(qm_ref, mk_ref, mv_ref, om_ref)


def prompt_attention(qkv, memkv, batch, seq, sinks_layer):
    nq = seq // Q_TILE
    k_col = Q_W // HEAD_DIM
    v_col = (Q_W + KV_W) // HEAD_DIM
    qm_col = (Q_W + 2 * KV_W) // HEAD_DIM
    smem = pl.BlockSpec(memory_space=pltpu.SMEM)
    tensor_specs = [
        pl.BlockSpec((Q_TILE, GROUP * HEAD_DIM), lambda b, h, i: (b * nq + i, h)),
        pl.BlockSpec((seq, HEAD_DIM), lambda b, h, i: (b, k_col + h)),
        pl.BlockSpec((seq, HEAD_DIM), lambda b, h, i: (b, v_col + h)),
        pl.BlockSpec((Q_TILE, HEAD_DIM), lambda b, h, i: (b * nq + i, qm_col + h)),
        pl.BlockSpec((N_MEM, HEAD_DIM), lambda b, h, i: (b, h)),
        pl.BlockSpec((N_MEM, HEAD_DIM), lambda b, h, i: (b, N_MEM_HEADS + h)),
    ]
    out_specs = [
        pl.BlockSpec((Q_TILE, GROUP * HEAD_DIM), lambda b, h, i: (b * nq + i, h)),
        pl.BlockSpec((Q_TILE, HEAD_DIM), lambda b, h, i: (b * nq + i, h)),
    ]
    out_shape = [jax.ShapeDtypeStruct((batch * seq, Q_W), BF16),
                 jax.ShapeDtypeStruct((batch * seq, MQ_W), BF16)]
    width = GROUP * Q_TILE
    row_scratch = pltpu.VMEM((1, width), F32)
    slopes = jnp.asarray(SLOPES)
    tensors = (qkv, qkv, qkv, qkv, memkv, memkv)
    if sinks_layer is None:
        return pl.pallas_call(
            _moba_prompt_kernel,
            grid=(batch, N_KV_HEADS, nq),
            in_specs=[smem] + tensor_specs,
            out_specs=out_specs, out_shape=out_shape,
            scratch_shapes=[
                pltpu.VMEM((seq, HEAD_DIM), BF16),
                pltpu.VMEM((seq // MOBA_BLOCK, HEAD_DIM, MOBA_BLOCK), BF16),
                pltpu.VMEM((seq // MOBA_BLOCK, HEAD_DIM), F32),
                pltpu.VMEM((MOBA_BLOCK, width), F32),
                row_scratch,
                pltpu.VMEM((seq // MOBA_BLOCK, width), F32),
            ],
            compiler_params=_params(3),
            name="moba_prompt",
        )(slopes, *tensors)
    return pl.pallas_call(
        _swa_prompt_kernel,
        grid=(batch, N_KV_HEADS, nq),
        in_specs=[smem, smem] + tensor_specs,
        out_specs=out_specs, out_shape=out_shape,
        scratch_shapes=[
            pltpu.VMEM((seq, HEAD_DIM), BF16),
            pltpu.VMEM((seq // WINDOW, HEAD_DIM, WINDOW), BF16),
            row_scratch,
            row_scratch,
        ],
        compiler_params=_params(3),
        name="swa_prompt",
    )(slopes, sinks_layer, *tensors)


def _head_rows(ref, head, n_tokens, n_heads):
    return ref[pl.ds(head, n_tokens, stride=n_heads), :]


def _row_scores(k, q_row):
    return jnp.sum(k * q_row, axis=-1, keepdims=True) * SCALE


def _sample_memory_kernel(qm_ref, mk_ref, mv_ref, o_ref):
    for h in range(N_MEM_HEADS):
        lanes = slice(h * HEAD_DIM, (h + 1) * HEAD_DIM)
        s = _row_scores(_head_rows(mk_ref, h, N_MEM, N_MEM_HEADS), qm_ref[:, lanes])
        p = jnp.exp(s - jnp.max(s, axis=0, keepdims=True))
        l = jnp.sum(p, axis=0, keepdims=True)
        v = _head_rows(mv_ref, h, N_MEM, N_MEM_HEADS)
        o_ref[:, lanes] = jnp.sum(p * v, axis=0, keepdims=True) / l


def sample_memory_attention(qm, mem_k, mem_v, layer):
    b = qm.shape[0]
    row = pl.BlockSpec((None, 1, MQ_W), lambda i: (i, 0, 0))
    cache = pl.BlockSpec((None, None, N_MEM * N_MEM_HEADS, HEAD_DIM), lambda i: (layer, i, 0, 0))
    return pl.pallas_call(
        _sample_memory_kernel,
        grid=(b,),
        in_specs=[row, cache, cache],
        out_specs=row,
        out_shape=jax.ShapeDtypeStruct((b, 1, MQ_W), F32),
        compiler_params=_params(1),
        name="sample_memory",
    )(qm, mem_k, mem_v)


def _sample_swa_kernel(sinks_ref, q_ref, kc_ref, vc_ref, kn_ref, vn_ref, o_ref):
    back = (WINDOW - lax.broadcasted_iota(jnp.int32, (WINDOW, 1), 0)).astype(F32)
    for hq in range(N_HEADS):
        kv = hq // GROUP
        q_row = q_ref[:, hq * HEAD_DIM:(hq + 1) * HEAD_DIM]
        lanes = slice(kv * HEAD_DIM, (kv + 1) * HEAD_DIM)
        sink = sinks_ref[hq]
        s_c = (_row_scores(_head_rows(kc_ref, kv, WINDOW, N_KV_HEADS), q_row)
               - float(SLOPES[hq]) * back)
        s_n = _row_scores(kn_ref[:, lanes], q_row)
        m = jnp.maximum(jnp.maximum(jnp.max(s_c, axis=0, keepdims=True), s_n), sink)
        p_c = jnp.exp(s_c - m)
        p_n = jnp.exp(s_n - m)
        l = jnp.sum(p_c, axis=0, keepdims=True) + p_n + jnp.exp(sink - m)
        v_c = _head_rows(vc_ref, kv, WINDOW, N_KV_HEADS)
        o = jnp.sum(p_c * v_c, axis=0, keepdims=True) + p_n * vn_ref[:, lanes]
        o_ref[:, hq * HEAD_DIM:(hq + 1) * HEAD_DIM] = o / l


def sample_swa_attention(q, k_new, v_new, cache_k, cache_v, sinks_layer, layer_b):
    b = q.shape[0]
    q_spec = pl.BlockSpec((None, 1, Q_W), lambda i: (i, 0, 0))
    kv_row = pl.BlockSpec((None, 1, KV_W), lambda i: (i, 0, 0))
    cache = pl.BlockSpec((None, None, WINDOW * N_KV_HEADS, HEAD_DIM),
                         lambda i: (layer_b, i, 0, 0))
    return pl.pallas_call(
        _sample_swa_kernel,
        grid=(b,),
        in_specs=[pl.BlockSpec(memory_space=pltpu.SMEM), q_spec, cache, cache, kv_row, kv_row],
        out_specs=q_spec,
        out_shape=jax.ShapeDtypeStruct((b, 1, Q_W), F32),
        compiler_params=_params(1),
        name="sample_swa",
    )(sinks_layer, q, cache_k, cache_v, k_new, v_new)


PAGE_ROWS = PAGE_SIZE * N_KV_HEADS
SUBLANES = 8


def _moba_gate_kernel(pt_ref, q_ref, *refs):
    del pt_ref
    pages = refs[:GATE_PAGES_PER_STEP]
    sel_ref, kmean_ref = refs[GATE_PAGES_PER_STEP:]
    s = pl.program_id(1)
    blocks_per_step = GATE_PAGES_PER_STEP // PAGES_PER_BLOCK
    n_blocks = kmean_ref.shape[1]

    sub = lax.broadcasted_iota(jnp.int32, (blocks_per_step, HEAD_DIM), 0)
    means = [jnp.zeros((blocks_per_step, HEAD_DIM), F32) for _ in range(N_KV_HEADS)]
    for r in range(blocks_per_step):
        total = None
        for page in pages[r * PAGES_PER_BLOCK:(r + 1) * PAGES_PER_BLOCK]:
            part = jnp.sum(page[...].reshape(PAGE_ROWS // SUBLANES, SUBLANES, HEAD_DIM), axis=0)
            total = part if total is None else total + part
        for kv in range(N_KV_HEADS):
            head_sum = total[kv:kv + 1, :] + total[kv + N_KV_HEADS:kv + N_KV_HEADS + 1, :]
            means[kv] = jnp.where(sub == r, head_sum * (1.0 / MOBA_BLOCK), means[kv])
    first = pl.multiple_of(s * blocks_per_step, blocks_per_step)
    for kv in range(N_KV_HEADS):
        kmean_ref[kv, pl.ds(first, blocks_per_step), :] = means[kv]

    @pl.when(s == pl.num_programs(1) - 1)
    def _():
        q = q_ref[...]
        head = lax.broadcasted_iota(jnp.int32, (N_HEADS, n_blocks), 0)
        gate = jnp.zeros((N_HEADS, n_blocks), F32)
        for kv in range(N_KV_HEADS):
            g_kv = _dot_t(q, kmean_ref[kv], precision=lax.Precision.HIGHEST)
            gate = jnp.where((head >= kv * GROUP) & (head < (kv + 1) * GROUP), g_kv, gate)
        blk = lax.broadcasted_iota(jnp.int32, (N_HEADS, n_blocks), 1).astype(F32)
        lane = lax.broadcasted_iota(jnp.int32, sel_ref.shape, 1)
        picks = jnp.zeros(sel_ref.shape, F32)
        for t in range(MOBA_TOPK):
            best = jnp.max(gate, axis=-1, keepdims=True)
            idx = jnp.min(jnp.where(gate == best, blk, float(n_blocks)), axis=-1, keepdims=True)
            picks = jnp.where(lane == t, idx, picks)
            gate = jnp.where(blk == idx, -jnp.inf, gate)
        sel_ref[...] = picks.astype(jnp.int32)


def sample_moba_select(q, cache_k, page_table, layer_a):
    b = q.shape[0]
    n_pages = page_table.shape[1]
    n_blocks = n_pages // PAGES_PER_BLOCK
    steps = n_pages // GATE_PAGES_PER_STEP

    def page_spec(r):
        return pl.BlockSpec(
            (None, None, PAGE_ROWS, HEAD_DIM),
            lambda i, s, pt: (layer_a, pt[i, s * GATE_PAGES_PER_STEP + r], 0, 0))

    grid_spec = pltpu.PrefetchScalarGridSpec(
        num_scalar_prefetch=1,
        grid=(b, steps),
        in_specs=[pl.BlockSpec((None, N_HEADS, HEAD_DIM), lambda i, s, pt: (i, 0, 0))]
        + [page_spec(r) for r in range(GATE_PAGES_PER_STEP)],
        out_specs=pl.BlockSpec((None, N_HEADS, HEAD_DIM), lambda i, s, pt: (i, 0, 0)),
        scratch_shapes=[pltpu.VMEM((N_KV_HEADS, n_blocks, HEAD_DIM), F32)],
    )
    sel = pl.pallas_call(
        _moba_gate_kernel,
        grid_spec=grid_spec,
        out_shape=jax.ShapeDtypeStruct((b, N_HEADS, HEAD_DIM), jnp.int32),
        compiler_params=_params(2),
        name="sample_moba_select",
    )(page_table, q, *([cache_k] * GATE_PAGES_PER_STEP))
    return sel[:, :, :MOBA_TOPK]


N_SEL_PAGES = GROUP * MOBA_TOPK * PAGES_PER_BLOCK


def _moba_sample_kernel(pt_ref, sel_ref, slopes_ref, q_ref, kn_ref, vn_ref, *refs, past_len):
    del pt_ref
    k_pages = refs[:N_SEL_PAGES]
    v_pages = refs[N_SEL_PAGES:2 * N_SEL_PAGES]
    o_ref = refs[2 * N_SEL_PAGES]
    b = pl.program_id(0)
    kvh = pl.program_id(1)
    within = lax.broadcasted_iota(jnp.int32, (PAGE_SIZE, 1), 0)
    for g in range(GROUP):
        head = kvh * GROUP + g
        slope = slopes_ref[head]
        q_row = q_ref[g:g + 1, :]
        s_own = _row_scores(kn_ref[...], q_row)
        scores = []
        m = s_own
        for t in range(MOBA_TOPK):
            first = sel_ref[(b * N_HEADS + head) * MOBA_TOPK + t] * MOBA_BLOCK
            for r in range(PAGES_PER_BLOCK):
                page = k_pages[(g * MOBA_TOPK + t) * PAGES_PER_BLOCK + r]
                dist = (past_len - (first + r * PAGE_SIZE) - within).astype(F32)
                s = _row_scores(_head_rows(page, kvh, PAGE_SIZE, N_KV_HEADS), q_row) - slope * dist
                scores.append(s)
                m = jnp.maximum(m, jnp.max(s, axis=0, keepdims=True))
        p_own = jnp.exp(s_own - m)
        l = p_own
        o = p_own * vn_ref[...]
        for idx, s in enumerate(scores):
            p = jnp.exp(s - m)
            l = l + jnp.sum(p, axis=0, keepdims=True)
            v = _head_rows(v_pages[g * MOBA_TOPK * PAGES_PER_BLOCK + idx], kvh, PAGE_SIZE, N_KV_HEADS)
            o = o + jnp.sum(p * v, axis=0, keepdims=True)
        o_ref[g:g + 1, :] = o / l


def sample_moba_attention(q, k_new, v_new, cache_k, cache_v, page_table, sel, layer_a):
    b = q.shape[0]
    past_len = page_table.shape[1] * PAGE_SIZE

    def page_spec(g, t, r):
        def index_map(i, h, pt, sl):
            block = sl[(i * N_HEADS + h * GROUP + g) * MOBA_TOPK + t]
            return (layer_a, pt[i, block * PAGES_PER_BLOCK + r], 0, 0)
        return pl.BlockSpec((None, None, PAGE_ROWS, HEAD_DIM), index_map)

    page_specs = [page_spec(g, t, r) for g in range(GROUP) for t in range(MOBA_TOPK)
                  for r in range(PAGES_PER_BLOCK)]
    q_spec = pl.BlockSpec((None, None, GROUP, HEAD_DIM), lambda i, h, pt, sl: (i, h, 0, 0))
    row_spec = pl.BlockSpec((None, None, 1, HEAD_DIM), lambda i, h, pt, sl: (i, h, 0, 0))
    grid_spec = pltpu.PrefetchScalarGridSpec(
        num_scalar_prefetch=2,
        grid=(b, N_KV_HEADS),
        in_specs=[pl.BlockSpec(memory_space=pltpu.SMEM), q_spec, row_spec, row_spec]
        + page_specs + page_specs,
        out_specs=q_spec,
    )
    return pl.pallas_call(
        functools.partial(_moba_sample_kernel, past_len=past_len),
        grid_spec=grid_spec,
        out_shape=jax.ShapeDtypeStruct((b, N_KV_HEADS, GROUP, HEAD_DIM), F32),
        compiler_params=_params(2),
        name="sample_moba",
    )(page_table, sel.reshape(-1), jnp.asarray(SLOPES), q, k_new, v_new,
      *([cache_k] * N_SEL_PAGES), *([cache_v] * N_SEL_PAGES))


DENSE_TM = 1024
DOWN_TM = 512


def _half_ffn(x, xg, ssq, wg_all, wu_all, wd_all, layer, m, next_gain):
    a = ffn_up(xg, ssq, wg_all, wu_all, layer, m, tm=DENSE_TM, tn=512)
    return matmul_residual([a], None, wd_all, layer, x, 0.5, m, tm=DOWN_TM, tn=512,
                           next_gain=next_gain)


def kernel(x_prompt, x_sample, cache_moba_k, cache_moba_v, cache_swa_k, cache_swa_v, cache_mem_k, cache_mem_v, page_table, mem_prompt, g_ffn1, w_ffn1_gate, w_ffn1_up, w_ffn1_down, g_attn, w_in, w_out, sinks, g_mem, w_mem_kv, g_ffn2, w_ffn2_gate, w_ffn2_up, w_ffn2_down, g_final):
    bp, seq, d = x_prompt.shape
    bs = x_sample.shape[0]
    m = bp * seq
    n_pool = cache_moba_k.shape[1]
    mem = mem_prompt.reshape(bp * N_MEM, d)
    mem_k_cache = cache_mem_k.reshape(DEPTH, bs, N_MEM * N_MEM_HEADS, HEAD_DIM)
    mem_v_cache = cache_mem_v.reshape(DEPTH, bs, N_MEM * N_MEM_HEADS, HEAD_DIM)
    swa_k_cache = cache_swa_k.reshape(-1, bs, WINDOW * N_KV_HEADS, HEAD_DIM)
    swa_v_cache = cache_swa_v.reshape(-1, bs, WINDOW * N_KV_HEADS, HEAD_DIM)
    moba_k_flat = cache_moba_k.reshape(-1, n_pool, PAGE_ROWS, HEAD_DIM)
    moba_v_flat = cache_moba_v.reshape(-1, n_pool, PAGE_ROWS, HEAD_DIM)

    moba_kp, moba_vp, moba_ks, moba_vs = [], [], [], []
    swa_kp, swa_vp, swa_ks, swa_vs = [], [], [], []
    mem_kp, mem_vp = [], []
    x, xg, ssq = stream_start(x_prompt.reshape(m, d), x_sample.reshape(bs, d), g_ffn1, 0,
                              tm=DOWN_TM)
    for l in range(DEPTH):
        x, xg, ssq = _half_ffn(x, xg, ssq, w_ffn1_gate, w_ffn1_up, w_ffn1_down, l, m,
                               (g_attn, l))

        qkv = norm_project(xg, ssq, w_in, l, m, tm=DENSE_TM, tn=1024)
        qkv_s = qkv[m:]
        memkv = project(rms_norm(mem, g_mem, l, BF16), w_mem_kv, l, tn=512)
        mem_kp.append(memkv[:, :MQ_W].reshape(bp, N_MEM, N_MEM_HEADS, HEAD_DIM))
        mem_vp.append(memkv[:, MQ_W:].reshape(bp, N_MEM, N_MEM_HEADS, HEAD_DIM))

        kp = qkv[:m, Q_W:Q_W + KV_W].reshape(bp, seq, N_KV_HEADS, HEAD_DIM)
        vp = qkv[:m, Q_W + KV_W:Q_W + 2 * KV_W].reshape(bp, seq, N_KV_HEADS, HEAD_DIM)
        qs = qkv_s[:, :Q_W]
        ks = qkv_s[:, Q_W:Q_W + KV_W]
        vs = qkv_s[:, Q_W + KV_W:Q_W + 2 * KV_W]
        qms = qkv_s[:, Q_W + 2 * KV_W:]
        ks4 = ks.reshape(bs, 1, N_KV_HEADS, HEAD_DIM)
        vs4 = vs.reshape(bs, 1, N_KV_HEADS, HEAD_DIM)

        oms = sample_memory_attention(qms.reshape(bs, 1, MQ_W), mem_k_cache, mem_v_cache, l)
        j = l // 2
        if l % 2 == 0:
            osp, omp = prompt_attention(qkv, memkv, bp, seq, None)
            sel = sample_moba_select(qs.reshape(bs, N_HEADS, HEAD_DIM), moba_k_flat, page_table, j)
            oss = sample_moba_attention(
                qs.reshape(bs, N_KV_HEADS, GROUP, HEAD_DIM),
                ks.reshape(bs, N_KV_HEADS, 1, HEAD_DIM), vs.reshape(bs, N_KV_HEADS, 1, HEAD_DIM),
                moba_k_flat, moba_v_flat, page_table, sel, j)
            moba_kp.append(kp)
            moba_vp.append(vp)
            moba_ks.append(ks4)
            moba_vs.append(vs4)
        else:
            osp, omp = prompt_attention(qkv, memkv, bp, seq, sinks[j])
            oss = sample_swa_attention(qs.reshape(bs, 1, Q_W), ks.reshape(bs, 1, KV_W),
                                       vs.reshape(bs, 1, KV_W), swa_k_cache, swa_v_cache,
                                       sinks[j], j)
            swa_kp.append(kp[:, -WINDOW:])
            swa_vp.append(vp[:, -WINDOW:])
            swa_ks.append(jnp.concatenate([cache_swa_k[j], ks4], axis=1)[:, -WINDOW:])
            swa_vs.append(jnp.concatenate([cache_swa_v[j], vs4], axis=1)[:, -WINDOW:])

        x, xg, ssq = matmul_residual(
            [osp, omp], [oss.reshape(bs, Q_W), oms.reshape(bs, MQ_W)], w_out, l, x, 1.0, m,
            tm=DENSE_TM, tn=1024, next_gain=(g_ffn2, l))

        next_gain = (g_ffn1, l + 1) if l + 1 < DEPTH else None
        out = _half_ffn(x, xg, ssq, w_ffn2_gate, w_ffn2_up, w_ffn2_down, l, m, next_gain)
        if next_gain is None:
            (x,) = out
        else:
            x, xg, ssq = out

    y_prompt, y_sample = stream_end(x, g_final, m, tm=DOWN_TM)
    return (y_prompt.reshape(bp, seq, d), y_sample.reshape(bs, 1, d),
            jnp.stack(moba_kp), jnp.stack(moba_vp), jnp.stack(moba_ks), jnp.stack(moba_vs),
            jnp.stack(swa_kp), jnp.stack(swa_vp), jnp.stack(swa_ks), jnp.stack(swa_vs),
            jnp.stack(mem_kp), jnp.stack(mem_vp))
```

```python
import functools
import math

import jax
import jax.numpy as jnp
import numpy as np
from jax import lax
from jax.experimental import pallas as pl
from jax.experimental.pallas import tpu as pltpu

F32 = jnp.float32
BF16 = jnp.bfloat16

D_MODEL = 2048
DEPTH = 4
PAGE_SIZE = 128
HEAD_DIM = 128
N_HEADS = 12
N_KV_HEADS = 4
GROUP = N_HEADS // N_KV_HEADS
N_MEM_HEADS = 4
N_MEM = 256
D_FF = 5632
MOBA_BLOCK = 256
MOBA_TOPK = 3
WINDOW = 128
RMS_EPS = 1e-6
Q_W = N_HEADS * HEAD_DIM
KV_W = N_KV_HEADS * HEAD_DIM
MQ_W = N_MEM_HEADS * HEAD_DIM
IN_W = Q_W + 2 * KV_W + MQ_W
SCALE = HEAD_DIM ** -0.5
PAGES_PER_BLOCK = MOBA_BLOCK // PAGE_SIZE

VMEM_LIMIT_BYTES = 56 * 1024 * 1024
MASKED = -1e30
Q_TILE = 256
GATE_PAGES_PER_STEP = 32


def _alibi_slope_list(n):
    def pow2(m):
        start = 2.0 ** (-(2.0 ** -(math.log2(m) - 3)))
        return [start ** (i + 1) for i in range(m)]
    if math.log2(n).is_integer():
        return pow2(n)
    c = 2 ** math.floor(math.log2(n))
    return pow2(c) + _alibi_slope_list(2 * c)[0::2][:n - c]


SLOPES = np.array(_alibi_slope_list(N_HEADS), dtype=np.float32)


def _params(n_grid_dims):
    return pltpu.CompilerParams(
        dimension_semantics=("arbitrary",) * n_grid_dims,
        vmem_limit_bytes=VMEM_LIMIT_BYTES)


def _dot_t(a, b, precision=None):
    return lax.dot_general(a, b, (((1,), (1,)), ((), ())), precision=precision,
                           preferred_element_type=F32)


def _norm_kernel(x_ref, g_ref, o_ref):
    x = x_ref[...]
    ms = jnp.mean(x * x, axis=-1, keepdims=True)
    o_ref[...] = (x * lax.rsqrt(ms + RMS_EPS) * g_ref[...]).astype(o_ref.dtype)


def rms_norm(x, g_all, layer, out_dtype):
    m, d = x.shape
    tm = min(m, 512)
    g3 = g_all.reshape(g_all.shape[0], 1, d)
    return pl.pallas_call(
        _norm_kernel,
        grid=(m // tm,),
        in_specs=[pl.BlockSpec((tm, d), lambda i: (i, 0)),
                  pl.BlockSpec((None, 1, d), lambda i: (layer, 0, 0))],
        out_specs=pl.BlockSpec((tm, d), lambda i: (i, 0)),
        out_shape=jax.ShapeDtypeStruct((m, d), out_dtype),
        compiler_params=_params(1),
        name="rms_norm",
    )(x, g3)


def _proj_kernel(h_ref, w_ref, o_ref, wb_ref):
    @pl.when(pl.program_id(1) == 0)
    def _():
        wb_ref[...] = w_ref[...].astype(BF16)
    o_ref[...] = jnp.dot(h_ref[...], wb_ref[...], preferred_element_type=F32)


def project(h, w_all, layer, tn):
    m, k = h.shape
    n = w_all.shape[2]
    tm = min(m, 1024)
    return pl.pallas_call(
        _proj_kernel,
        grid=(n // tn, m // tm),
        in_specs=[pl.BlockSpec((tm, k), lambda j, i: (i, 0)),
                  pl.BlockSpec((None, k, tn), lambda j, i: (layer, 0, j))],
        out_specs=pl.BlockSpec((tm, tn), lambda j, i: (i, j)),
        out_shape=jax.ShapeDtypeStruct((m, n), F32),
        scratch_shapes=[pltpu.VMEM((k, tn), BF16)],
        compiler_params=_params(2),
        name="project",
    )(h, w_all)


LANES = 128


def _for_row_tiles(i, n_full, tail, body):
    @pl.when(i < n_full)
    def _():
        body(slice(None))

    @pl.when(i == n_full)
    def _():
        body(slice(0, tail))


def _lane_partial_sums(x):
    total = x[:, :LANES]
    for c in range(1, x.shape[1] // LANES):
        total = total + x[:, c * LANES:(c + 1) * LANES]
    return total


def _store_norm_inputs(x, g_ref, xg_ref, ssq_ref, rows):
    xg_ref[rows, :] = (x * g_ref[...]).astype(xg_ref.dtype)
    ssq_ref[rows, :] = _lane_partial_sums(x * x)


def _inv_rms(ssq, d_model):
    total = jnp.sum(jnp.sum(ssq, axis=0), axis=-1, keepdims=True)
    return lax.rsqrt(total * (1.0 / d_model) + RMS_EPS)


def _stream_start_kernel(xp_ref, xs_ref, g_ref, x_ref, xg_ref, ssq_ref, *, n_full, tail):
    def emit(src_ref):
        def body(rows):
            x = src_ref[rows, :]
            x_ref[rows, :] = x
            _store_norm_inputs(x, g_ref, xg_ref, ssq_ref, rows)
        return body
    i = pl.program_id(0)

    @pl.when(i < n_full)
    def _():
        emit(xp_ref)(slice(None))

    @pl.when(i == n_full)
    def _():
        emit(xs_ref)(slice(0, tail))


def stream_start(x_main, x_tail, g_all, layer, tm):
    m, d = x_main.shape
    tail = x_tail.shape[0]
    n_full = m // tm
    rows = m + tail
    g3 = g_all.reshape(g_all.shape[0], 1, d)
    row_block = lambda i: (i, 0)
    return pl.pallas_call(
        functools.partial(_stream_start_kernel, n_full=n_full, tail=tail),
        grid=(n_full + 1,),
        in_specs=[pl.BlockSpec((tm, d), lambda i: (jnp.minimum(i, n_full - 1), 0)),
                  pl.BlockSpec((tail, d), lambda i: (0, 0)),
                  pl.BlockSpec((None, 1, d), lambda i: (layer, 0, 0))],
        out_specs=[pl.BlockSpec((tm, d), row_block), pl.BlockSpec((tm, d), row_block),
                   pl.BlockSpec((None, tm, LANES), lambda i: (0, i, 0))],
        out_shape=[jax.ShapeDtypeStruct((rows, d), F32), jax.ShapeDtypeStruct((rows, d), BF16),
                   jax.ShapeDtypeStruct((1, rows, LANES), F32)],
        compiler_params=_params(1),
        name="stream_start",
    )(x_main, x_tail, g3)


def _stream_end_kernel(x_ref, g_ref, yp_ref, ys_ref, *, n_full, tail):
    def normed(rows):
        x = x_ref[rows, :]
        ms = jnp.mean(x * x, axis=-1, keepdims=True)
        return x * lax.rsqrt(ms + RMS_EPS) * g_ref[...]
    i = pl.program_id(0)

    @pl.when(i < n_full)
    def _():
        yp_ref[...] = normed(slice(None))

    @pl.when(i == n_full)
    def _():
        ys_ref[...] = normed(slice(0, tail))


def stream_end(x, g, m, tm):
    rows, d = x.shape
    tail = rows - m
    n_full = m // tm
    return pl.pallas_call(
        functools.partial(_stream_end_kernel, n_full=n_full, tail=tail),
        grid=(n_full + 1,),
        in_specs=[pl.BlockSpec((tm, d), lambda i: (i, 0)),
                  pl.BlockSpec((1, d), lambda i: (0, 0))],
        out_specs=[pl.BlockSpec((tm, d), lambda i: (jnp.minimum(i, n_full - 1), 0)),
                   pl.BlockSpec((tail, d), lambda i: (0, 0))],
        out_shape=[jax.ShapeDtypeStruct((m, d), F32), jax.ShapeDtypeStruct((tail, d), F32)],
        compiler_params=_params(1),
        name="stream_end",
    )(x, g.reshape(1, d))


def _normed_specs(nj, tm, k):
    return [pl.BlockSpec((tm, k), lambda j, i: (i, 0)),
            pl.BlockSpec((nj, tm, LANES), lambda j, i: (0, jnp.where(j == 0, i, 0), 0))]


def _cached_inv_rms(ssq_ref, inv_ref, j, i, rows, d_model):
    @pl.when(j == 0)
    def _():
        inv_ref[i, rows, :] = _inv_rms(ssq_ref[:, rows, :], d_model)
    return inv_ref[i, rows, :]


def _norm_proj_kernel(xg_ref, ssq_ref, w_ref, o_ref, wb_ref, inv_ref, *, n_full, tail):
    j = pl.program_id(0)
    i = pl.program_id(1)

    @pl.when(i == 0)
    def _():
        wb_ref[...] = w_ref[...].astype(BF16)

    def body(rows):
        inv = _cached_inv_rms(ssq_ref, inv_ref, j, i, rows, xg_ref.shape[1])
        o_ref[rows, :] = inv * jnp.dot(xg_ref[rows, :], wb_ref[...], preferred_element_type=F32)
    _for_row_tiles(i, n_full, tail, body)


def norm_project(xg, ssq, w_all, layer, m, tm, tn):
    rows, k = xg.shape
    n = w_all.shape[2]
    n_full = m // tm
    return pl.pallas_call(
        functools.partial(_norm_proj_kernel, n_full=n_full, tail=rows - m),
        grid=(n // tn, n_full + 1),
        in_specs=_normed_specs(ssq.shape[0], tm, k)
        + [pl.BlockSpec((None, k, tn), lambda j, i: (layer, 0, j))],
        out_specs=pl.BlockSpec((tm, tn), lambda j, i: (i, j)),
        out_shape=jax.ShapeDtypeStruct((rows, n), F32),
        scratch_shapes=[pltpu.VMEM((k, tn), BF16), pltpu.VMEM((n_full + 1, tm, 1), F32)],
        compiler_params=_params(2),
        name="norm_project",
    )(xg, ssq, w_all)


def _ffn_up_kernel(xg_ref, ssq_ref, wg_ref, wu_ref, a_ref, wgb_ref, wub_ref, inv_ref,
                   *, n_full, tail):
    j = pl.program_id(0)
    i = pl.program_id(1)

    @pl.when(i == 0)
    def _():
        wgb_ref[...] = wg_ref[...].astype(BF16)
        wub_ref[...] = wu_ref[...].astype(BF16)

    def body(rows):
        inv = _cached_inv_rms(ssq_ref, inv_ref, j, i, rows, xg_ref.shape[1])
        h = xg_ref[rows, :]
        g = inv * jnp.dot(h, wgb_ref[...], preferred_element_type=F32)
        u = inv * jnp.dot(h, wub_ref[...], preferred_element_type=F32)
        a_ref[rows, :] = (g * jax.nn.sigmoid(g) * u).astype(a_ref.dtype)
    _for_row_tiles(i, n_full, tail, body)


def ffn_up(xg, ssq, wg_all, wu_all, layer, m, tm, tn):
    rows, k = xg.shape
    n = wg_all.shape[2]
    n_full = m // tm
    w_spec = pl.BlockSpec((None, k, tn), lambda j, i: (layer, 0, j))
    return pl.pallas_call(
        functools.partial(_ffn_up_kernel, n_full=n_full, tail=rows - m),
        grid=(n // tn, n_full + 1),
        in_specs=_normed_specs(ssq.shape[0], tm, k) + [w_spec, w_spec],
        out_specs=pl.BlockSpec((tm, tn), lambda j, i: (i, j)),
        out_shape=jax.ShapeDtypeStruct((rows, n), BF16),
        scratch_shapes=[pltpu.VMEM((k, tn), BF16), pltpu.VMEM((k, tn), BF16),
                        pltpu.VMEM((n_full + 1, tm, 1), F32)],
        compiler_params=_params(2),
        name="ffn_up",
    )(xg, ssq, wg_all, wu_all)


def _mm_res_kernel(*refs, k_sizes, split_tail, emit_norm_inputs, scale, n_full, tail):
    n_a = len(k_sizes)
    main_refs = refs[:n_a]
    tail_refs = refs[n_a:2 * n_a] if split_tail else main_refs
    rest = refs[2 * n_a:] if split_tail else refs[n_a:]
    if emit_norm_inputs:
        w_ref, x_ref, g_ref, o_ref, xg_ref, ssq_ref, wb_ref = rest
    else:
        w_ref, x_ref, o_ref, wb_ref = rest
    i = pl.program_id(1)

    @pl.when(i == 0)
    def _():
        wb_ref[...] = w_ref[...].astype(BF16)

    def body(rows):
        a_refs = main_refs if rows == slice(None) else tail_refs
        acc = None
        off = 0
        for a_ref, ksz in zip(a_refs, k_sizes):
            d = jnp.dot(a_ref[rows, :].astype(BF16), wb_ref[off:off + ksz, :],
                        preferred_element_type=F32)
            acc = d if acc is None else acc + d
            off += ksz
        x = x_ref[rows, :] + scale * acc
        o_ref[rows, :] = x
        if emit_norm_inputs:
            _store_norm_inputs(x, g_ref, xg_ref, ssq_ref, rows)
    _for_row_tiles(i, n_full, tail, body)


def matmul_residual(a_main, a_tail, w_all, layer, x, scale, m, tm, tn, next_gain=None):
    rows, n = x.shape
    tail = rows - m
    n_full = m // tm
    k_sizes = tuple(a.shape[1] for a in a_main)
    k = sum(k_sizes)
    nj = n // tn
    split_tail = a_tail is not None
    if split_tail:
        a_specs = [pl.BlockSpec((tm, ks), lambda j, i: (jnp.minimum(i, n_full - 1), 0))
                   for ks in k_sizes]
        a_specs += [pl.BlockSpec((tail, ks), lambda j, i: (0, 0)) for ks in k_sizes]
        operands = list(a_main) + list(a_tail)
    else:
        a_specs = [pl.BlockSpec((tm, ks), lambda j, i: (i, 0)) for ks in k_sizes]
        operands = list(a_main)
    tile = pl.BlockSpec((tm, tn), lambda j, i: (i, j))
    in_specs = a_specs + [pl.BlockSpec((None, k, tn), lambda j, i: (layer, 0, j)), tile]
    operands += [w_all, x]
    out_specs = [tile]
    out_shape = [jax.ShapeDtypeStruct((rows, n), F32)]
    emit = next_gain is not None
    if emit:
        g_all, g_layer = next_gain
        in_specs.append(pl.BlockSpec((None, 1, tn), lambda j, i: (g_layer, 0, j)))
        operands.append(g_all.reshape(-1, 1, n))
        out_specs += [tile, pl.BlockSpec((None, tm, LANES), lambda j, i: (j, i, 0))]
        out_shape += [jax.ShapeDtypeStruct((rows, n), BF16),
                      jax.ShapeDtypeStruct((nj, rows, LANES), F32)]
    return pl.pallas_call(
        functools.partial(_mm_res_kernel, k_sizes=k_sizes, split_tail=split_tail,
                          emit_norm_inputs=emit, scale=scale, n_full=n_full, tail=tail),
        grid=(nj, n_full + 1),
        in_specs=in_specs, out_specs=out_specs, out_shape=out_shape,
        scratch_shapes=[pltpu.VMEM((k, tn), BF16)],
        compiler_params=_params(2),
        name="matmul_residual",
    )(*operands)


def _softmax_pv(s, v):
    m = jnp.max(s, axis=-1, keepdims=True)
    p = jnp.exp(s - m)
    l = jnp.sum(p, axis=-1, keepdims=True)
    return jnp.dot(p.astype(BF16), v, preferred_element_type=F32), m, l


def _memory_attention_tile(qm_ref, mk_ref, mv_ref, om_ref):
    qm = (qm_ref[...] * SCALE).astype(BF16)
    s = _dot_t(qm, mk_ref[...].astype(BF16))
    o, _, l = _softmax_pv(s, mv_ref[...].astype(BF16))
    om_ref[...] = (o / l).astype(om_ref.dtype)


def _moba_prompt_kernel(slopes_ref, q_ref, k_ref, v_ref, qm_ref, mk_ref, mv_ref,
                        os_ref, om_ref, kb_ref, vt_ref, kmean_ref, bias_ref, nslope_ref,
                        chosen_ref, sa_ref, sb_ref, acc_ref, m_ref, l_ref):
    kvh = pl.program_id(1)
    i = pl.program_id(2)
    nb = kmean_ref.shape[0]
    tq = q_ref.shape[0]
    width = GROUP * tq
    key = lax.broadcasted_iota(jnp.int32, (MOBA_BLOCK, width), 0)
    query = lax.broadcasted_iota(jnp.int32, (MOBA_BLOCK, width), 1) & (tq - 1)

    @pl.when(i == 0)
    def _():
        kb_ref[...] = k_ref[...].astype(BF16)
        for n in range(nb):
            rows = slice(n * MOBA_BLOCK, (n + 1) * MOBA_BLOCK)
            vt_ref[n] = v_ref[rows, :].T.astype(BF16)
            kmean_ref[n:n + 1, :] = jnp.mean(k_ref[rows, :], axis=0, keepdims=True)
        lane = lax.broadcasted_iota(jnp.int32, (1, width), 1)
        nslope = jnp.zeros((1, width), F32)
        for g in range(GROUP):
            nslope = jnp.where((lane >= g * tq) & (lane < (g + 1) * tq),
                               -slopes_ref[kvh * GROUP + g], nslope)
        nslope_ref[...] = nslope
        bias_ref[...] = nslope * (query - key).astype(F32)

    q_t = jnp.concatenate(
        [q_ref[:, g * HEAD_DIM:(g + 1) * HEAD_DIM] for g in range(GROUP)], axis=0).T
    gate = jnp.dot(kmean_ref[...], q_t, precision=lax.Precision.HIGHEST,
                   preferred_element_type=F32)
    blk = lax.broadcasted_iota(jnp.int32, (nb, width), 0)
    rank = jnp.zeros((nb, width), jnp.int32)
    for m in range(nb - 1):
        gm = gate[m:m + 1, :]
        beats = (gm > gate) | ((gm == gate) & (m < blk))
        rank = rank + jnp.where(beats & (m < i), 1, 0)
    chosen_ref[...] = jnp.where((blk < i) & (rank < MOBA_TOPK), 1.0, 0.0)

    qb_t = (q_t * SCALE).astype(BF16)

    def scores(block):
        start = pl.multiple_of(block * MOBA_BLOCK, MOBA_BLOCK)
        return jnp.dot(kb_ref[pl.ds(start, MOBA_BLOCK), :], qb_t, preferred_element_type=F32)

    sa_ref[...] = scores(0)
    s = jnp.where(query >= key, scores(i) + bias_ref[...], MASKED)
    m_own = jnp.max(s, axis=0, keepdims=True)
    p = jnp.exp(s - m_own)
    m_ref[...] = m_own
    l_ref[...] = jnp.sum(p, axis=0, keepdims=True)
    acc_ref[...] = jnp.dot(vt_ref[i], p.astype(BF16), preferred_element_type=F32)

    def absorb(s_ref, block):
        offset = ((i - block) * MOBA_BLOCK).astype(F32)
        shift = jnp.where(chosen_ref[pl.ds(block, 1), :] > 0.0, nslope_ref[...] * offset, MASKED)
        s = s_ref[...] + bias_ref[...] + shift
        m_old = m_ref[...]
        m_new = jnp.maximum(m_old, jnp.max(s, axis=0, keepdims=True))
        alpha = jnp.exp(m_old - m_new)
        p = jnp.exp(s - m_new)
        m_ref[...] = m_new
        l_ref[...] = alpha * l_ref[...] + jnp.sum(p, axis=0, keepdims=True)
        acc_ref[...] = alpha * acc_ref[...] + jnp.dot(vt_ref[block], p.astype(BF16),
                                                      preferred_element_type=F32)

    def past_pair(t, carry):
        first = 2 * t
        second = jnp.minimum(first + 1, nb - 1)
        sb_ref[...] = scores(second)
        absorb(sa_ref, first)
        sa_ref[...] = scores(jnp.minimum(first + 2, nb - 1))
        absorb(sb_ref, second)
        return carry

    lax.fori_loop(0, (i + 1) // 2, past_pair, 0)
    out_t = acc_ref[...] / l_ref[...]
    for g in range(GROUP):
        os_ref[:, g * HEAD_DIM:(g + 1) * HEAD_DIM] = (
            out_t[:, g * tq:(g + 1) * tq].T.astype(os_ref.dtype))

    _memory_attention_tile(qm_ref, mk_ref, mv_ref, om_ref)


def _swa_prompt_kernel(slopes_ref, sinks_ref, q_ref, k_ref, v_ref, qm_ref, mk_ref, mv_ref,
                       os_ref, om_ref, kb_ref, vt_ref, bias_ref, sink_ref):
    kvh = pl.program_id(1)
    i = pl.program_id(2)
    tq = q_ref.shape[0]
    width = GROUP * tq
    n_chunks = tq // WINDOW + 1
    span = n_chunks * WINDOW

    @pl.when(i == 0)
    def _():
        kb_ref[...] = k_ref[...].astype(BF16)
        for n in range(vt_ref.shape[0]):
            vt_ref[n] = v_ref[n * WINDOW:(n + 1) * WINDOW, :].T.astype(BF16)
        lane = lax.broadcasted_iota(jnp.int32, (1, width), 1)
        nslope = jnp.zeros((1, width), F32)
        sink = jnp.zeros((1, width), F32)
        for g in range(GROUP):
            in_head = (lane >= g * tq) & (lane < (g + 1) * tq)
            nslope = jnp.where(in_head, -slopes_ref[kvh * GROUP + g], nslope)
            sink = jnp.where(in_head, sinks_ref[kvh * GROUP + g], sink)
        sink_ref[...] = sink
        key = lax.broadcasted_iota(jnp.int32, (span, width), 0)
        query = lax.broadcasted_iota(jnp.int32, (span, width), 1) & (tq - 1)
        for placement, lead in enumerate((0, WINDOW)):
            dist = query - key + lead
            bias_ref[placement] = jnp.where((dist >= 0) & (dist <= WINDOW),
                                            nslope * dist.astype(F32), MASKED)

    first_chunk = jnp.maximum(i * (tq // WINDOW) - 1, 0)
    kstart = pl.multiple_of(first_chunk * WINDOW, WINDOW)
    q_t = jnp.concatenate(
        [q_ref[:, g * HEAD_DIM:(g + 1) * HEAD_DIM] for g in range(GROUP)], axis=0).T
    qb_t = (q_t * SCALE).astype(BF16)
    s = (jnp.dot(kb_ref[pl.ds(kstart, span), :], qb_t, preferred_element_type=F32)
         + bias_ref[jnp.minimum(i, 1)])
    sink = sink_ref[...]
    m = jnp.maximum(jnp.max(s, axis=0, keepdims=True), sink)
    p = jnp.exp(s - m)
    l = jnp.sum(p, axis=0, keepdims=True) + jnp.exp(sink - m)
    pb = p.astype(BF16)
    out_t = None
    for c in range(n_chunks):
        part = jnp.dot(vt_ref[first_chunk + c], pb[c * WINDOW:(c + 1) * WINDOW, :],
                       preferred_element_type=F32)
        out_t = part if out_t is None else out_t + part
    out_t = out_t / l
    for g in range(GROUP):
        os_ref[:, g * HEAD_DIM:(g + 1) * HEAD_DIM] = (
            out_t[:, g * tq:(g + 1) * tq].T.astype(os_ref.dtype))

    _memory_attention_tile(qm_ref, mk_ref, mv_ref, om_ref)


def prompt_attention(qkv, memkv, batch, seq, sinks_layer):
    nq = seq // Q_TILE
    k_col = Q_W // HEAD_DIM
    v_col = (Q_W + KV_W) // HEAD_DIM
    qm_col = (Q_W + 2 * KV_W) // HEAD_DIM
    smem = pl.BlockSpec(memory_space=pltpu.SMEM)
    tensor_specs = [
        pl.BlockSpec((Q_TILE, GROUP * HEAD_DIM), lambda b, h, i: (b * nq + i, h)),
        pl.BlockSpec((seq, HEAD_DIM), lambda b, h, i: (b, k_col + h)),
        pl.BlockSpec((seq, HEAD_DIM), lambda b, h, i: (b, v_col + h)),
        pl.BlockSpec((Q_TILE, HEAD_DIM), lambda b, h, i: (b * nq + i, qm_col + h)),
        pl.BlockSpec((N_MEM, HEAD_DIM), lambda b, h, i: (b, h)),
        pl.BlockSpec((N_MEM, HEAD_DIM), lambda b, h, i: (b, N_MEM_HEADS + h)),
    ]
    out_specs = [
        pl.BlockSpec((Q_TILE, GROUP * HEAD_DIM), lambda b, h, i: (b * nq + i, h)),
        pl.BlockSpec((Q_TILE, HEAD_DIM), lambda b, h, i: (b * nq + i, h)),
    ]
    out_shape = [jax.ShapeDtypeStruct((batch * seq, Q_W), BF16),
                 jax.ShapeDtypeStruct((batch * seq, MQ_W), BF16)]
    width = GROUP * Q_TILE
    row_scratch = pltpu.VMEM((1, width), F32)
    slopes = jnp.asarray(SLOPES)
    tensors = (qkv, qkv, qkv, qkv, memkv, memkv)
    if sinks_layer is None:
        return pl.pallas_call(
            _moba_prompt_kernel,
            grid=(batch, N_KV_HEADS, nq),
            in_specs=[smem] + tensor_specs,
            out_specs=out_specs, out_shape=out_shape,
            scratch_shapes=[
                pltpu.VMEM((seq, HEAD_DIM), BF16),
                pltpu.VMEM((seq // MOBA_BLOCK, HEAD_DIM, MOBA_BLOCK), BF16),
                pltpu.VMEM((seq // MOBA_BLOCK, HEAD_DIM), F32),
                pltpu.VMEM((MOBA_BLOCK, width), F32),
                row_scratch,
                pltpu.VMEM((seq // MOBA_BLOCK, width), F32),
                pltpu.VMEM((MOBA_BLOCK, width), F32),
                pltpu.VMEM((MOBA_BLOCK, width), F32),
                pltpu.VMEM((HEAD_DIM, width), F32),
                row_scratch,
                row_scratch,
            ],
            compiler_params=_params(3),
            name="moba_prompt",
        )(slopes, *tensors)
    return pl.pallas_call(
        _swa_prompt_kernel,
        grid=(batch, N_KV_HEADS, nq),
        in_specs=[smem, smem] + tensor_specs,
        out_specs=out_specs, out_shape=out_shape,
        scratch_shapes=[
            pltpu.VMEM((seq, HEAD_DIM), BF16),
            pltpu.VMEM((seq // WINDOW, HEAD_DIM, WINDOW), BF16),
            pltpu.VMEM((2, Q_TILE + WINDOW, width), F32),
            row_scratch,
        ],
        compiler_params=_params(3),
        name="swa_prompt",
    )(slopes, sinks_layer, *tensors)


def _export_kv_kernel(k0_ref, v0_ref, k1_ref, v1_ref, ok_ref, ov_ref):
    def export(k_ref, v_ref):
        for h in range(N_KV_HEADS):
            lanes = slice(h * HEAD_DIM, (h + 1) * HEAD_DIM)
            rows = pl.ds(h, k_ref.shape[0], stride=N_KV_HEADS)
            ok_ref[rows, :] = k_ref[:, lanes]
            ov_ref[rows, :] = v_ref[:, lanes]

    @pl.when(pl.program_id(0) == 0)
    def _():
        export(k0_ref, v0_ref)

    @pl.when(pl.program_id(0) == 1)
    def _():
        export(k1_ref, v1_ref)


def export_prompt_kv(qkv_pair, m, tm):
    k_col = Q_W // KV_W
    v_col = k_col + 1

    def spec(layer, col):
        return pl.BlockSpec((tm, KV_W), lambda a, i: (jnp.where(a == layer, i, 0), col))

    out_spec = pl.BlockSpec((None, tm * N_KV_HEADS, HEAD_DIM), lambda a, i: (a, i, 0))
    out = jax.ShapeDtypeStruct((2, m * N_KV_HEADS, HEAD_DIM), F32)
    q0, q1 = qkv_pair
    return pl.pallas_call(
        _export_kv_kernel,
        grid=(2, m // tm),
        in_specs=[spec(0, k_col), spec(0, v_col), spec(1, k_col), spec(1, v_col)],
        out_specs=[out_spec, out_spec], out_shape=[out, out],
        compiler_params=_params(2),
        name="export_prompt_kv",
    )(q0, q0, q1, q1)


def _head_rows(ref, head, n_tokens, n_heads):
    return ref[pl.ds(head, n_tokens, stride=n_heads), :]


def _row_scores(k, q_row):
    return jnp.sum(k * q_row, axis=-1, keepdims=True) * SCALE


def _sample_memory_kernel(qm_ref, mk_ref, mv_ref, o_ref):
    for h in range(N_MEM_HEADS):
        lanes = slice(h * HEAD_DIM, (h + 1) * HEAD_DIM)
        s = _row_scores(_head_rows(mk_ref, h, N_MEM, N_MEM_HEADS), qm_ref[:, lanes])
        p = jnp.exp(s - jnp.max(s, axis=0, keepdims=True))
        l = jnp.sum(p, axis=0, keepdims=True)
        v = _head_rows(mv_ref, h, N_MEM, N_MEM_HEADS)
        o_ref[:, lanes] = jnp.sum(p * v, axis=0, keepdims=True) / l


def sample_memory_attention(qm, mem_k, mem_v, layer):
    b = qm.shape[0]
    row = pl.BlockSpec((None, 1, MQ_W), lambda i: (i, 0, 0))
    cache = pl.BlockSpec((None, None, N_MEM * N_MEM_HEADS, HEAD_DIM), lambda i: (layer, i, 0, 0))
    return pl.pallas_call(
        _sample_memory_kernel,
        grid=(b,),
        in_specs=[row, cache, cache],
        out_specs=row,
        out_shape=jax.ShapeDtypeStruct((b, 1, MQ_W), F32),
        compiler_params=_params(1),
        name="sample_memory",
    )(qm, mem_k, mem_v)


def _sample_swa_kernel(sinks_ref, q_ref, kc_ref, vc_ref, kn_ref, vn_ref, o_ref):
    back = (WINDOW - lax.broadcasted_iota(jnp.int32, (WINDOW, 1), 0)).astype(F32)
    for hq in range(N_HEADS):
        kv = hq // GROUP
        q_row = q_ref[:, hq * HEAD_DIM:(hq + 1) * HEAD_DIM]
        lanes = slice(kv * HEAD_DIM, (kv + 1) * HEAD_DIM)
        sink = sinks_ref[hq]
        s_c = (_row_scores(_head_rows(kc_ref, kv, WINDOW, N_KV_HEADS), q_row)
               - float(SLOPES[hq]) * back)
        s_n = _row_scores(kn_ref[:, lanes], q_row)
        m = jnp.maximum(jnp.maximum(jnp.max(s_c, axis=0, keepdims=True), s_n), sink)
        p_c = jnp.exp(s_c - m)
        p_n = jnp.exp(s_n - m)
        l = jnp.sum(p_c, axis=0, keepdims=True) + p_n + jnp.exp(sink - m)
        v_c = _head_rows(vc_ref, kv, WINDOW, N_KV_HEADS)
        o = jnp.sum(p_c * v_c, axis=0, keepdims=True) + p_n * vn_ref[:, lanes]
        o_ref[:, hq * HEAD_DIM:(hq + 1) * HEAD_DIM] = o / l


def sample_swa_attention(q, k_new, v_new, cache_k, cache_v, sinks_layer, layer_b):
    b = q.shape[0]
    q_spec = pl.BlockSpec((None, 1, Q_W), lambda i: (i, 0, 0))
    kv_row = pl.BlockSpec((None, 1, KV_W), lambda i: (i, 0, 0))
    cache = pl.BlockSpec((None, None, WINDOW * N_KV_HEADS, HEAD_DIM),
                         lambda i: (layer_b, i, 0, 0))
    return pl.pallas_call(
        _sample_swa_kernel,
        grid=(b,),
        in_specs=[pl.BlockSpec(memory_space=pltpu.SMEM), q_spec, cache, cache, kv_row, kv_row],
        out_specs=q_spec,
        out_shape=jax.ShapeDtypeStruct((b, 1, Q_W), F32),
        compiler_params=_params(1),
        name="sample_swa",
    )(sinks_layer, q, cache_k, cache_v, k_new, v_new)


PAGE_ROWS = PAGE_SIZE * N_KV_HEADS
SUBLANES = 8


def _moba_gate_kernel(pt_ref, q_ref, *refs):
    del pt_ref
    pages = refs[:GATE_PAGES_PER_STEP]
    sel_ref, kmean_ref = refs[GATE_PAGES_PER_STEP:]
    s = pl.program_id(1)
    blocks_per_step = GATE_PAGES_PER_STEP // PAGES_PER_BLOCK
    n_blocks = kmean_ref.shape[1]

    sub = lax.broadcasted_iota(jnp.int32, (blocks_per_step, HEAD_DIM), 0)
    means = [jnp.zeros((blocks_per_step, HEAD_DIM), F32) for _ in range(N_KV_HEADS)]
    for r in range(blocks_per_step):
        total = None
        for page in pages[r * PAGES_PER_BLOCK:(r + 1) * PAGES_PER_BLOCK]:
            part = jnp.sum(page[...].reshape(PAGE_ROWS // SUBLANES, SUBLANES, HEAD_DIM), axis=0)
            total = part if total is None else total + part
        for kv in range(N_KV_HEADS):
            head_sum = total[kv:kv + 1, :] + total[kv + N_KV_HEADS:kv + N_KV_HEADS + 1, :]
            means[kv] = jnp.where(sub == r, head_sum * (1.0 / MOBA_BLOCK), means[kv])
    first = pl.multiple_of(s * blocks_per_step, blocks_per_step)
    for kv in range(N_KV_HEADS):
        kmean_ref[kv, pl.ds(first, blocks_per_step), :] = means[kv]

    @pl.when(s == pl.num_programs(1) - 1)
    def _():
        q = q_ref[...]
        head = lax.broadcasted_iota(jnp.int32, (N_HEADS, n_blocks), 0)
        gate = jnp.zeros((N_HEADS, n_blocks), F32)
        for kv in range(N_KV_HEADS):
            g_kv = _dot_t(q, kmean_ref[kv], precision=lax.Precision.HIGHEST)
            gate = jnp.where((head >= kv * GROUP) & (head < (kv + 1) * GROUP), g_kv, gate)
        blk = lax.broadcasted_iota(jnp.int32, (N_HEADS, n_blocks), 1).astype(F32)
        lane = lax.broadcasted_iota(jnp.int32, sel_ref.shape, 1)
        picks = jnp.zeros(sel_ref.shape, F32)
        for t in range(MOBA_TOPK):
            best = jnp.max(gate, axis=-1, keepdims=True)
            idx = jnp.min(jnp.where(gate == best, blk, float(n_blocks)), axis=-1, keepdims=True)
            picks = jnp.where(lane == t, idx, picks)
            gate = jnp.where(blk == idx, -jnp.inf, gate)
        sel_ref[...] = picks.astype(jnp.int32)


def sample_moba_select(q, cache_k, page_table, layer_a):
    b = q.shape[0]
    n_pages = page_table.shape[1]
    n_blocks = n_pages // PAGES_PER_BLOCK
    steps = n_pages // GATE_PAGES_PER_STEP

    def page_spec(r):
        return pl.BlockSpec(
            (None, None, PAGE_ROWS, HEAD_DIM),
            lambda i, s, pt: (layer_a, pt[i, s * GATE_PAGES_PER_STEP + r], 0, 0))

    grid_spec = pltpu.PrefetchScalarGridSpec(
        num_scalar_prefetch=1,
        grid=(b, steps),
        in_specs=[pl.BlockSpec((None, N_HEADS, HEAD_DIM), lambda i, s, pt: (i, 0, 0))]
        + [page_spec(r) for r in range(GATE_PAGES_PER_STEP)],
        out_specs=pl.BlockSpec((None, N_HEADS, HEAD_DIM), lambda i, s, pt: (i, 0, 0)),
        scratch_shapes=[pltpu.VMEM((N_KV_HEADS, n_blocks, HEAD_DIM), F32)],
    )
    sel = pl.pallas_call(
        _moba_gate_kernel,
        grid_spec=grid_spec,
        out_shape=jax.ShapeDtypeStruct((b, N_HEADS, HEAD_DIM), jnp.int32),
        compiler_params=_params(2),
        name="sample_moba_select",
    )(page_table, q, *([cache_k] * GATE_PAGES_PER_STEP))
    return sel[:, :, :MOBA_TOPK]


N_SEL_PAGES = GROUP * MOBA_TOPK * PAGES_PER_BLOCK


def _moba_sample_kernel(pt_ref, sel_ref, slopes_ref, q_ref, kn_ref, vn_ref, *refs, past_len):
    del pt_ref
    k_pages = refs[:N_SEL_PAGES]
    v_pages = refs[N_SEL_PAGES:2 * N_SEL_PAGES]
    o_ref = refs[2 * N_SEL_PAGES]
    b = pl.program_id(0)
    kvh = pl.program_id(1)
    within = lax.broadcasted_iota(jnp.int32, (PAGE_SIZE, 1), 0)
    for g in range(GROUP):
        head = kvh * GROUP + g
        slope = slopes_ref[head]
        q_row = q_ref[g:g + 1, :]
        s_own = _row_scores(kn_ref[...], q_row)
        scores = []
        m = s_own
        for t in range(MOBA_TOPK):
            first = sel_ref[(b * N_HEADS + head) * MOBA_TOPK + t] * MOBA_BLOCK
            for r in range(PAGES_PER_BLOCK):
                page = k_pages[(g * MOBA_TOPK + t) * PAGES_PER_BLOCK + r]
                dist = (past_len - (first + r * PAGE_SIZE) - within).astype(F32)
                s = _row_scores(_head_rows(page, kvh, PAGE_SIZE, N_KV_HEADS), q_row) - slope * dist
                scores.append(s)
                m = jnp.maximum(m, jnp.max(s, axis=0, keepdims=True))
        p_own = jnp.exp(s_own - m)
        l = p_own
        o = p_own * vn_ref[...]
        for idx, s in enumerate(scores):
            p = jnp.exp(s - m)
            l = l + jnp.sum(p, axis=0, keepdims=True)
            v = _head_rows(v_pages[g * MOBA_TOPK * PAGES_PER_BLOCK + idx], kvh, PAGE_SIZE, N_KV_HEADS)
            o = o + jnp.sum(p * v, axis=0, keepdims=True)
        o_ref[g:g + 1, :] = o / l


def sample_moba_attention(q, k_new, v_new, cache_k, cache_v, page_table, sel, layer_a):
    b = q.shape[0]
    past_len = page_table.shape[1] * PAGE_SIZE

    def page_spec(g, t, r):
        def index_map(i, h, pt, sl):
            block = sl[(i * N_HEADS + h * GROUP + g) * MOBA_TOPK + t]
            return (layer_a, pt[i, block * PAGES_PER_BLOCK + r], 0, 0)
        return pl.BlockSpec((None, None, PAGE_ROWS, HEAD_DIM), index_map)

    page_specs = [page_spec(g, t, r) for g in range(GROUP) for t in range(MOBA_TOPK)
                  for r in range(PAGES_PER_BLOCK)]
    q_spec = pl.BlockSpec((None, None, GROUP, HEAD_DIM), lambda i, h, pt, sl: (i, h, 0, 0))
    row_spec = pl.BlockSpec((None, None, 1, HEAD_DIM), lambda i, h, pt, sl: (i, h, 0, 0))
    grid_spec = pltpu.PrefetchScalarGridSpec(
        num_scalar_prefetch=2,
        grid=(b, N_KV_HEADS),
        in_specs=[pl.BlockSpec(memory_space=pltpu.SMEM), q_spec, row_spec, row_spec]
        + page_specs + page_specs,
        out_specs=q_spec,
    )
    return pl.pallas_call(
        functools.partial(_moba_sample_kernel, past_len=past_len),
        grid_spec=grid_spec,
        out_shape=jax.ShapeDtypeStruct((b, N_KV_HEADS, GROUP, HEAD_DIM), F32),
        compiler_params=_params(2),
        name="sample_moba",
    )(page_table, sel.reshape(-1), jnp.asarray(SLOPES), q, k_new, v_new,
      *([cache_k] * N_SEL_PAGES), *([cache_v] * N_SEL_PAGES))


DENSE_TM = 1024
DOWN_TM = 512


def _half_ffn(x, xg, ssq, wg_all, wu_all, wd_all, layer, m, next_gain):
    a = ffn_up(xg, ssq, wg_all, wu_all, layer, m, tm=DENSE_TM, tn=512)
    return matmul_residual([a], None, wd_all, layer, x, 0.5, m, tm=DOWN_TM, tn=512,
                           next_gain=next_gain)


def kernel(x_prompt, x_sample, cache_moba_k, cache_moba_v, cache_swa_k, cache_swa_v, cache_mem_k, cache_mem_v, page_table, mem_prompt, g_ffn1, w_ffn1_gate, w_ffn1_up, w_ffn1_down, g_attn, w_in, w_out, sinks, g_mem, w_mem_kv, g_ffn2, w_ffn2_gate, w_ffn2_up, w_ffn2_down, g_final):
    bp, seq, d = x_prompt.shape
    bs = x_sample.shape[0]
    m = bp * seq
    n_pool = cache_moba_k.shape[1]
    mem = mem_prompt.reshape(bp * N_MEM, d)
    mem_k_cache = cache_mem_k.reshape(DEPTH, bs, N_MEM * N_MEM_HEADS, HEAD_DIM)
    mem_v_cache = cache_mem_v.reshape(DEPTH, bs, N_MEM * N_MEM_HEADS, HEAD_DIM)
    swa_k_cache = cache_swa_k.reshape(-1, bs, WINDOW * N_KV_HEADS, HEAD_DIM)
    swa_v_cache = cache_swa_v.reshape(-1, bs, WINDOW * N_KV_HEADS, HEAD_DIM)
    moba_k_flat = cache_moba_k.reshape(-1, n_pool, PAGE_ROWS, HEAD_DIM)
    moba_v_flat = cache_moba_v.reshape(-1, n_pool, PAGE_ROWS, HEAD_DIM)

    moba_qkv, moba_ks, moba_vs = [], [], []
    swa_kp, swa_vp, swa_ks, swa_vs = [], [], [], []
    mem_kp, mem_vp = [], []
    x, xg, ssq = stream_start(x_prompt.reshape(m, d), x_sample.reshape(bs, d), g_ffn1, 0,
                              tm=DOWN_TM)
    for l in range(DEPTH):
        x, xg, ssq = _half_ffn(x, xg, ssq, w_ffn1_gate, w_ffn1_up, w_ffn1_down, l, m,
                               (g_attn, l))

        qkv = norm_project(xg, ssq, w_in, l, m, tm=DENSE_TM, tn=1024)
        qkv_s = qkv[m:]
        memkv = project(rms_norm(mem, g_mem, l, BF16), w_mem_kv, l, tn=512)
        mem_kp.append(memkv[:, :MQ_W].reshape(bp, N_MEM, N_MEM_HEADS, HEAD_DIM))
        mem_vp.append(memkv[:, MQ_W:].reshape(bp, N_MEM, N_MEM_HEADS, HEAD_DIM))

        qs = qkv_s[:, :Q_W]
        ks = qkv_s[:, Q_W:Q_W + KV_W]
        vs = qkv_s[:, Q_W + KV_W:Q_W + 2 * KV_W]
        qms = qkv_s[:, Q_W + 2 * KV_W:]
        ks4 = ks.reshape(bs, 1, N_KV_HEADS, HEAD_DIM)
        vs4 = vs.reshape(bs, 1, N_KV_HEADS, HEAD_DIM)

        oms = sample_memory_attention(qms.reshape(bs, 1, MQ_W), mem_k_cache, mem_v_cache, l)
        j = l // 2
        if l % 2 == 0:
            osp, omp = prompt_attention(qkv, memkv, bp, seq, None)
            sel = sample_moba_select(qs.reshape(bs, N_HEADS, HEAD_DIM), moba_k_flat, page_table, j)
            oss = sample_moba_attention(
                qs.reshape(bs, N_KV_HEADS, GROUP, HEAD_DIM),
                ks.reshape(bs, N_KV_HEADS, 1, HEAD_DIM), vs.reshape(bs, N_KV_HEADS, 1, HEAD_DIM),
                moba_k_flat, moba_v_flat, page_table, sel, j)
            moba_qkv.append(qkv)
            moba_ks.append(ks4)
            moba_vs.append(vs4)
        else:
            osp, omp = prompt_attention(qkv, memkv, bp, seq, sinks[j])
            oss = sample_swa_attention(qs.reshape(bs, 1, Q_W), ks.reshape(bs, 1, KV_W),
                                       vs.reshape(bs, 1, KV_W), swa_k_cache, swa_v_cache,
                                       sinks[j], j)
            last = qkv[:m].reshape(bp, seq, IN_W)[:, -WINDOW:]
            swa_kp.append(last[:, :, Q_W:Q_W + KV_W].reshape(bp, WINDOW, N_KV_HEADS, HEAD_DIM))
            swa_vp.append(last[:, :, Q_W + KV_W:Q_W + 2 * KV_W].reshape(
                bp, WINDOW, N_KV_HEADS, HEAD_DIM))
            swa_ks.append(jnp.concatenate([cache_swa_k[j], ks4], axis=1)[:, -WINDOW:])
            swa_vs.append(jnp.concatenate([cache_swa_v[j], vs4], axis=1)[:, -WINDOW:])

        x, xg, ssq = matmul_residual(
            [osp, omp], [oss.reshape(bs, Q_W), oms.reshape(bs, MQ_W)], w_out, l, x, 1.0, m,
            tm=DENSE_TM, tn=1024, next_gain=(g_ffn2, l))

        next_gain = (g_ffn1, l + 1) if l + 1 < DEPTH else None
        out = _half_ffn(x, xg, ssq, w_ffn2_gate, w_ffn2_up, w_ffn2_down, l, m, next_gain)
        if next_gain is None:
            (x,) = out
        else:
            x, xg, ssq = out

    y_prompt, y_sample = stream_end(x, g_final, m, tm=DOWN_TM)
    moba_kp, moba_vp = export_prompt_kv(moba_qkv, m, tm=DOWN_TM)
    kv_shape = (len(moba_qkv), bp, seq, N_KV_HEADS, HEAD_DIM)
    return (y_prompt.reshape(bp, seq, d), y_sample.reshape(bs, 1, d),
            moba_kp.reshape(kv_shape), moba_vp.reshape(kv_shape),
            jnp.stack(moba_ks), jnp.stack(moba_vs),
            jnp.stack(swa_kp), jnp.stack(swa_vp), jnp.stack(swa_ks), jnp.stack(swa_vs),
            jnp.stack(mem_kp), jnp.stack(mem_vp))
```

```python
import functools
import math

import jax
import jax.numpy as jnp
import numpy as np
from jax import lax
from jax.experimental import pallas as pl
from jax.experimental.pallas import tpu as pltpu

F32 = jnp.float32
BF16 = jnp.bfloat16

D_MODEL = 2048
DEPTH = 4
PAGE_SIZE = 128
HEAD_DIM = 128
N_HEADS = 12
N_KV_HEADS = 4
GROUP = N_HEADS // N_KV_HEADS
N_MEM_HEADS = 4
N_MEM = 256
D_FF = 5632
MOBA_BLOCK = 256
MOBA_TOPK = 3
WINDOW = 128
RMS_EPS = 1e-6
Q_W = N_HEADS * HEAD_DIM
KV_W = N_KV_HEADS * HEAD_DIM
MQ_W = N_MEM_HEADS * HEAD_DIM
IN_W = Q_W + 2 * KV_W + MQ_W
SCALE = HEAD_DIM ** -0.5
PAGES_PER_BLOCK = MOBA_BLOCK // PAGE_SIZE

VMEM_LIMIT_BYTES = 56 * 1024 * 1024
MASKED = -1e30
Q_TILE = 256
GATE_PAGES_PER_STEP = 32


def _alibi_slope_list(n):
    def pow2(m):
        start = 2.0 ** (-(2.0 ** -(math.log2(m) - 3)))
        return [start ** (i + 1) for i in range(m)]
    if math.log2(n).is_integer():
        return pow2(n)
    c = 2 ** math.floor(math.log2(n))
    return pow2(c) + _alibi_slope_list(2 * c)[0::2][:n - c]


SLOPES = np.array(_alibi_slope_list(N_HEADS), dtype=np.float32)


def _params(n_grid_dims):
    return pltpu.CompilerParams(
        dimension_semantics=("arbitrary",) * n_grid_dims,
        vmem_limit_bytes=VMEM_LIMIT_BYTES)


def _dot_t(a, b, precision=None):
    return lax.dot_general(a, b, (((1,), (1,)), ((), ())), precision=precision,
                           preferred_element_type=F32)


def _norm_kernel(x_ref, g_ref, o_ref):
    x = x_ref[...]
    ms = jnp.mean(x * x, axis=-1, keepdims=True)
    o_ref[...] = (x * lax.rsqrt(ms + RMS_EPS) * g_ref[...]).astype(o_ref.dtype)


def rms_norm(x, g_all, layer, out_dtype):
    m, d = x.shape
    tm = min(m, 512)
    g3 = g_all.reshape(g_all.shape[0], 1, d)
    return pl.pallas_call(
        _norm_kernel,
        grid=(m // tm,),
        in_specs=[pl.BlockSpec((tm, d), lambda i: (i, 0)),
                  pl.BlockSpec((None, 1, d), lambda i: (layer, 0, 0))],
        out_specs=pl.BlockSpec((tm, d), lambda i: (i, 0)),
        out_shape=jax.ShapeDtypeStruct((m, d), out_dtype),
        compiler_params=_params(1),
        name="rms_norm",
    )(x, g3)


def _proj_kernel(h_ref, w_ref, o_ref, wb_ref):
    @pl.when(pl.program_id(1) == 0)
    def _():
        wb_ref[...] = w_ref[...].astype(BF16)
    o_ref[...] = jnp.dot(h_ref[...], wb_ref[...], preferred_element_type=F32)


def project(h, w_all, layer, tn):
    m, k = h.shape
    n = w_all.shape[2]
    tm = min(m, 1024)
    return pl.pallas_call(
        _proj_kernel,
        grid=(n // tn, m // tm),
        in_specs=[pl.BlockSpec((tm, k), lambda j, i: (i, 0)),
                  pl.BlockSpec((None, k, tn), lambda j, i: (layer, 0, j))],
        out_specs=pl.BlockSpec((tm, tn), lambda j, i: (i, j)),
        out_shape=jax.ShapeDtypeStruct((m, n), F32),
        scratch_shapes=[pltpu.VMEM((k, tn), BF16)],
        compiler_params=_params(2),
        name="project",
    )(h, w_all)


LANES = 128
EXACT = lax.Precision.HIGHEST


def _tail_first(i, n_full):
    return jnp.where(i == 0, n_full, i - 1)


def _main_tile(i):
    return jnp.maximum(i - 1, 0)


def _lane_partial_sums(x):
    total = x[:, :LANES]
    for c in range(1, x.shape[1] // LANES):
        total = total + x[:, c * LANES:(c + 1) * LANES]
    return total


def _store_norm_inputs(x, g_ref, xg_ref, ssq_ref):
    xg_ref[...] = (x * g_ref[...]).astype(xg_ref.dtype)
    ssq_ref[...] = _lane_partial_sums(x * x)


def _inv_rms(ssq, d_model):
    total = jnp.sum(jnp.sum(ssq, axis=0), axis=-1, keepdims=True)
    return lax.rsqrt(total * (1.0 / d_model) + RMS_EPS)


def _normed_rows(x, g):
    return x * lax.rsqrt(jnp.mean(x * x, axis=-1, keepdims=True) + RMS_EPS) * g


def _stream_start_kernel(xp_ref, xs_ref, g_ref, x_ref, xg_ref, ssq_ref, *, tail):
    i = pl.program_id(0)

    @pl.when(i == 0)
    def _():
        x_ref[0:tail, :] = xs_ref[...]

    @pl.when(i > 0)
    def _():
        x = xp_ref[...]
        x_ref[...] = x
        _store_norm_inputs(x, g_ref, xg_ref, ssq_ref)


def stream_start(x_main, x_tail, g_all, layer, tm):
    m, d = x_main.shape
    tail = x_tail.shape[0]
    n_full = m // tm
    g3 = g_all.reshape(g_all.shape[0], 1, d)
    main_block = lambda i: (_main_tile(i), 0)
    return pl.pallas_call(
        functools.partial(_stream_start_kernel, tail=tail),
        grid=(n_full + 1,),
        in_specs=[pl.BlockSpec((tm, d), main_block),
                  pl.BlockSpec((tail, d), lambda i: (0, 0)),
                  pl.BlockSpec((None, 1, d), lambda i: (layer, 0, 0))],
        out_specs=[pl.BlockSpec((tm, d), lambda i: (_tail_first(i, n_full), 0)),
                   pl.BlockSpec((tm, d), main_block),
                   pl.BlockSpec((None, tm, LANES), lambda i: (0, _main_tile(i), 0))],
        out_shape=[jax.ShapeDtypeStruct((m + tail, d), F32), jax.ShapeDtypeStruct((m, d), BF16),
                   jax.ShapeDtypeStruct((1, m, LANES), F32)],
        compiler_params=_params(1),
        name="stream_start",
    )(x_main, x_tail, g3)


def _stream_end_kernel(x_ref, g_ref, yp_ref, ys_ref, *, tail):
    i = pl.program_id(0)

    @pl.when(i == 0)
    def _():
        ys_ref[...] = _normed_rows(x_ref[0:tail, :], g_ref[...])

    @pl.when(i > 0)
    def _():
        yp_ref[...] = _normed_rows(x_ref[...], g_ref[...])


def stream_end(x, g, m, tm):
    rows, d = x.shape
    tail = rows - m
    n_full = m // tm
    return pl.pallas_call(
        functools.partial(_stream_end_kernel, tail=tail),
        grid=(n_full + 1,),
        in_specs=[pl.BlockSpec((tm, d), lambda i: (_tail_first(i, n_full), 0)),
                  pl.BlockSpec((1, d), lambda i: (0, 0))],
        out_specs=[pl.BlockSpec((tm, d), lambda i: (_main_tile(i), 0)),
                   pl.BlockSpec((tail, d), lambda i: (0, 0))],
        out_shape=[jax.ShapeDtypeStruct((m, d), F32), jax.ShapeDtypeStruct((tail, d), F32)],
        compiler_params=_params(1),
        name="stream_end",
    )(x, g.reshape(1, d))


def _normed_specs(x, gain, nj, tm, n_full):
    rows, k = x.shape
    tail = rows - n_full * tm
    g_all, g_layer = gain
    specs = [pl.BlockSpec((tm, k), lambda j, i: (_main_tile(i), 0)),
             pl.BlockSpec((nj, tm, LANES),
                          lambda j, i: (0, jnp.where(j == 0, _main_tile(i), n_full - 1), 0)),
             pl.BlockSpec((tail, k), lambda j, i: (n_full * tm // tail, 0)),
             pl.BlockSpec((None, 1, k), lambda j, i: (g_layer, 0, 0))]
    return specs, [x, g_all.reshape(-1, 1, k)]


def _cached_inv_rms(ssq_ref, inv_ref, j, i, d_model):
    @pl.when(j == 0)
    def _():
        inv_ref[i] = _inv_rms(ssq_ref[...], d_model)
    return inv_ref[i]


def _norm_proj_kernel(xg_ref, ssq_ref, xt_ref, g_ref, w_ref, o_ref, wb_ref, inv_ref, *, tail):
    j = pl.program_id(0)
    i = pl.program_id(1)

    @pl.when(i == 0)
    def _():
        wb_ref[...] = w_ref[...].astype(BF16)
        h = _normed_rows(xt_ref[...], g_ref[...])
        o_ref[0:tail, :] = jnp.dot(h, w_ref[...], precision=EXACT, preferred_element_type=F32)

    @pl.when(i > 0)
    def _():
        inv = _cached_inv_rms(ssq_ref, inv_ref, j, i, xg_ref.shape[1])
        o_ref[...] = inv * jnp.dot(xg_ref[...], wb_ref[...], preferred_element_type=F32)


def norm_project(xg, ssq, x, gain, w_all, layer, tm, tn):
    m, k = xg.shape
    rows = x.shape[0]
    n = w_all.shape[2]
    n_full = m // tm
    specs, operands = _normed_specs(x, gain, ssq.shape[0], tm, n_full)
    return pl.pallas_call(
        functools.partial(_norm_proj_kernel, tail=rows - m),
        grid=(n // tn, n_full + 1),
        in_specs=specs + [pl.BlockSpec((None, k, tn), lambda j, i: (layer, 0, j))],
        out_specs=pl.BlockSpec((tm, tn), lambda j, i: (_tail_first(i, n_full), j)),
        out_shape=jax.ShapeDtypeStruct((rows, n), F32),
        scratch_shapes=[pltpu.VMEM((k, tn), BF16), pltpu.VMEM((n_full + 1, tm, 1), F32)],
        compiler_params=_params(2),
        name="norm_project",
    )(xg, ssq, *operands, w_all)


def _ffn_up_kernel(xg_ref, ssq_ref, xt_ref, g_ref, wg_ref, wu_ref, a_ref, at_ref,
                   wgb_ref, wub_ref, inv_ref):
    j = pl.program_id(0)
    i = pl.program_id(1)

    def swiglu(g, u):
        return g * jax.nn.sigmoid(g) * u

    @pl.when(i == 0)
    def _():
        wgb_ref[...] = wg_ref[...].astype(BF16)
        wub_ref[...] = wu_ref[...].astype(BF16)
        h = _normed_rows(xt_ref[...], g_ref[...])
        at_ref[...] = swiglu(
            jnp.dot(h, wg_ref[...], precision=EXACT, preferred_element_type=F32),
            jnp.dot(h, wu_ref[...], precision=EXACT, preferred_element_type=F32))

    @pl.when(i > 0)
    def _():
        inv = _cached_inv_rms(ssq_ref, inv_ref, j, i, xg_ref.shape[1])
        h = xg_ref[...]
        g = inv * jnp.dot(h, wgb_ref[...], preferred_element_type=F32)
        u = inv * jnp.dot(h, wub_ref[...], preferred_element_type=F32)
        a_ref[...] = swiglu(g, u).astype(a_ref.dtype)


def ffn_up(xg, ssq, x, gain, wg_all, wu_all, layer, tm, tn):
    m, k = xg.shape
    tail = x.shape[0] - m
    n = wg_all.shape[2]
    n_full = m // tm
    specs, operands = _normed_specs(x, gain, ssq.shape[0], tm, n_full)
    w_spec = pl.BlockSpec((None, k, tn), lambda j, i: (layer, 0, j))
    return pl.pallas_call(
        _ffn_up_kernel,
        grid=(n // tn, n_full + 1),
        in_specs=specs + [w_spec, w_spec],
        out_specs=[pl.BlockSpec((tm, tn), lambda j, i: (_main_tile(i), j)),
                   pl.BlockSpec((tail, tn), lambda j, i: (0, j))],
        out_shape=[jax.ShapeDtypeStruct((m, n), BF16), jax.ShapeDtypeStruct((tail, n), F32)],
        scratch_shapes=[pltpu.VMEM((k, tn), BF16), pltpu.VMEM((k, tn), BF16),
                        pltpu.VMEM((n_full + 1, tm, 1), F32)],
        compiler_params=_params(2),
        name="ffn_up",
    )(xg, ssq, *operands, wg_all, wu_all)


def _mm_res_kernel(*refs, k_sizes, emit_norm_inputs, scale, tail):
    n_a = len(k_sizes)
    main_refs = refs[:n_a]
    tail_refs = refs[n_a:2 * n_a]
    if emit_norm_inputs:
        w_ref, x_ref, g_ref, o_ref, xg_ref, ssq_ref, wb_ref = refs[2 * n_a:]
    else:
        w_ref, x_ref, o_ref, wb_ref = refs[2 * n_a:]
    i = pl.program_id(1)

    def product(a_refs, w, cast, precision):
        acc = None
        off = 0
        for a_ref, ksz in zip(a_refs, k_sizes):
            d = jnp.dot(cast(a_ref[...]), w[off:off + ksz, :], precision=precision,
                        preferred_element_type=F32)
            acc = d if acc is None else acc + d
            off += ksz
        return acc

    @pl.when(i == 0)
    def _():
        wb_ref[...] = w_ref[...].astype(BF16)
        o_ref[0:tail, :] = x_ref[0:tail, :] + scale * product(tail_refs, w_ref, lambda a: a, EXACT)

    @pl.when(i > 0)
    def _():
        x = x_ref[...] + scale * product(main_refs, wb_ref, lambda a: a.astype(BF16), None)
        o_ref[...] = x
        if emit_norm_inputs:
            _store_norm_inputs(x, g_ref, xg_ref, ssq_ref)


def matmul_residual(a_main, a_tail, w_all, layer, x, scale, tm, tn, next_gain=None):
    rows, n = x.shape
    m = a_main[0].shape[0]
    tail = rows - m
    n_full = m // tm
    k_sizes = tuple(a.shape[1] for a in a_main)
    k = sum(k_sizes)
    nj = n // tn
    a_specs = [pl.BlockSpec((tm, ks), lambda j, i: (_main_tile(i), 0)) for ks in k_sizes]
    a_specs += [pl.BlockSpec((tail, ks), lambda j, i: (0, 0)) for ks in k_sizes]
    tile = pl.BlockSpec((tm, tn), lambda j, i: (_tail_first(i, n_full), j))
    in_specs = a_specs + [pl.BlockSpec((None, k, tn), lambda j, i: (layer, 0, j)), tile]
    operands = list(a_main) + list(a_tail) + [w_all, x]
    out_specs = [tile]
    out_shape = [jax.ShapeDtypeStruct((rows, n), F32)]
    emit = next_gain is not None
    if emit:
        g_all, g_layer = next_gain
        in_specs.append(pl.BlockSpec((None, 1, tn), lambda j, i: (g_layer, 0, j)))
        operands.append(g_all.reshape(-1, 1, n))
        out_specs += [pl.BlockSpec((tm, tn), lambda j, i: (_main_tile(i), j)),
                      pl.BlockSpec((None, tm, LANES), lambda j, i: (j, _main_tile(i), 0))]
        out_shape += [jax.ShapeDtypeStruct((m, n), BF16),
                      jax.ShapeDtypeStruct((nj, m, LANES), F32)]
    return pl.pallas_call(
        functools.partial(_mm_res_kernel, k_sizes=k_sizes, emit_norm_inputs=emit, scale=scale,
                          tail=tail),
        grid=(nj, n_full + 1),
        in_specs=in_specs, out_specs=out_specs, out_shape=out_shape,
        scratch_shapes=[pltpu.VMEM((k, tn), BF16)],
        compiler_params=_params(2),
        name="matmul_residual",
    )(*operands)


def _softmax_pv(s, v):
    m = jnp.max(s, axis=-1, keepdims=True)
    p = jnp.exp(s - m)
    l = jnp.sum(p, axis=-1, keepdims=True)
    return jnp.dot(p.astype(BF16), v, preferred_element_type=F32), m, l


def _memory_attention_tile(qm_ref, mk_ref, mv_ref, om_ref):
    qm = (qm_ref[...] * SCALE).astype(BF16)
    s = _dot_t(qm, mk_ref[...].astype(BF16))
    o, _, l = _softmax_pv(s, mv_ref[...].astype(BF16))
    om_ref[...] = (o / l).astype(om_ref.dtype)


def _moba_prompt_kernel(slopes_ref, q_ref, k_ref, v_ref, qm_ref, mk_ref, mv_ref,
                        os_ref, om_ref, kb_ref, vt_ref, kmean_ref, bias_ref, nslope_ref,
                        chosen_ref, sa_ref, sb_ref, acc_ref, m_ref, l_ref):
    kvh = pl.program_id(1)
    i = pl.program_id(2)
    nb = kmean_ref.shape[0]
    tq = q_ref.shape[0]
    width = GROUP * tq
    key = lax.broadcasted_iota(jnp.int32, (MOBA_BLOCK, width), 0)
    query = lax.broadcasted_iota(jnp.int32, (MOBA_BLOCK, width), 1) & (tq - 1)

    @pl.when(i == 0)
    def _():
        kb_ref[...] = k_ref[...].astype(BF16)
        for n in range(nb):
            rows = slice(n * MOBA_BLOCK, (n + 1) * MOBA_BLOCK)
            vt_ref[n] = v_ref[rows, :].T.astype(BF16)
            kmean_ref[n:n + 1, :] = jnp.mean(k_ref[rows, :], axis=0, keepdims=True)
        lane = lax.broadcasted_iota(jnp.int32, (1, width), 1)
        nslope = jnp.zeros((1, width), F32)
        for g in range(GROUP):
            nslope = jnp.where((lane >= g * tq) & (lane < (g + 1) * tq),
                               -slopes_ref[kvh * GROUP + g], nslope)
        nslope_ref[...] = nslope
        bias_ref[...] = nslope * (query - key).astype(F32)

    q_t = jnp.concatenate(
        [q_ref[:, g * HEAD_DIM:(g + 1) * HEAD_DIM] for g in range(GROUP)], axis=0).T
    gate = jnp.dot(kmean_ref[...], q_t, precision=lax.Precision.HIGHEST,
                   preferred_element_type=F32)
    blk = lax.broadcasted_iota(jnp.int32, (nb, width), 0)
    rank = jnp.zeros((nb, width), jnp.int32)
    for m in range(nb - 1):
        gm = gate[m:m + 1, :]
        beats = (gm > gate) | ((gm == gate) & (m < blk))
        rank = rank + jnp.where(beats & (m < i), 1, 0)
    chosen_ref[...] = jnp.where((blk < i) & (rank < MOBA_TOPK), 1.0, 0.0)

    qb_t = (q_t * SCALE).astype(BF16)

    def scores(block):
        start = pl.multiple_of(block * MOBA_BLOCK, MOBA_BLOCK)
        return jnp.dot(kb_ref[pl.ds(start, MOBA_BLOCK), :], qb_t, preferred_element_type=F32)

    sa_ref[...] = scores(0)
    s = jnp.where(query >= key, scores(i) + bias_ref[...], MASKED)
    m_own = jnp.max(s, axis=0, keepdims=True)
    p = jnp.exp(s - m_own)
    m_ref[...] = m_own
    l_ref[...] = jnp.sum(p, axis=0, keepdims=True)
    acc_ref[...] = jnp.dot(vt_ref[i], p.astype(BF16), preferred_element_type=F32)

    def absorb(s_ref, block):
        offset = ((i - block) * MOBA_BLOCK).astype(F32)
        shift = jnp.where(chosen_ref[pl.ds(block, 1), :] > 0.0, nslope_ref[...] * offset, MASKED)
        s = s_ref[...] + bias_ref[...] + shift
        m_old = m_ref[...]
        m_new = jnp.maximum(m_old, jnp.max(s, axis=0, keepdims=True))
        alpha = jnp.exp(m_old - m_new)
        p = jnp.exp(s - m_new)
        m_ref[...] = m_new
        l_ref[...] = alpha * l_ref[...] + jnp.sum(p, axis=0, keepdims=True)
        acc_ref[...] = alpha * acc_ref[...] + jnp.dot(vt_ref[block], p.astype(BF16),
                                                      preferred_element_type=F32)

    def past_pair(t, carry):
        first = 2 * t
        second = jnp.minimum(first + 1, nb - 1)
        sb_ref[...] = scores(second)
        absorb(sa_ref, first)
        sa_ref[...] = scores(jnp.minimum(first + 2, nb - 1))
        absorb(sb_ref, second)
        return carry

    lax.fori_loop(0, (i + 1) // 2, past_pair, 0)
    out_t = acc_ref[...] / l_ref[...]
    for g in range(GROUP):
        os_ref[:, g * HEAD_DIM:(g + 1) * HEAD_DIM] = (
            out_t[:, g * tq:(g + 1) * tq].T.astype(os_ref.dtype))

    _memory_attention_tile(qm_ref, mk_ref, mv_ref, om_ref)


def _swa_prompt_kernel(slopes_ref, sinks_ref, q_ref, k_ref, v_ref, qm_ref, mk_ref, mv_ref,
                       os_ref, om_ref, kb_ref, vt_ref, bias_ref, sink_ref):
    kvh = pl.program_id(1)
    i = pl.program_id(2)
    tq = q_ref.shape[0]
    width = GROUP * tq
    n_chunks = tq // WINDOW + 1
    span = n_chunks * WINDOW

    @pl.when(i == 0)
    def _():
        kb_ref[...] = k_ref[...].astype(BF16)
        for n in range(vt_ref.shape[0]):
            vt_ref[n] = v_ref[n * WINDOW:(n + 1) * WINDOW, :].T.astype(BF16)
        lane = lax.broadcasted_iota(jnp.int32, (1, width), 1)
        nslope = jnp.zeros((1, width), F32)
        sink = jnp.zeros((1, width), F32)
        for g in range(GROUP):
            in_head = (lane >= g * tq) & (lane < (g + 1) * tq)
            nslope = jnp.where(in_head, -slopes_ref[kvh * GROUP + g], nslope)
            sink = jnp.where(in_head, sinks_ref[kvh * GROUP + g], sink)
        sink_ref[...] = sink
        key = lax.broadcasted_iota(jnp.int32, (span, width), 0)
        query = lax.broadcasted_iota(jnp.int32, (span, width), 1) & (tq - 1)
        for placement, lead in enumerate((0, WINDOW)):
            dist = query - key + lead
            bias_ref[placement] = jnp.where((dist >= 0) & (dist <= WINDOW),
                                            nslope * dist.astype(F32), MASKED)

    first_chunk = jnp.maximum(i * (tq // WINDOW) - 1, 0)
    kstart = pl.multiple_of(first_chunk * WINDOW, WINDOW)
    q_t = jnp.concatenate(
        [q_ref[:, g * HEAD_DIM:(g + 1) * HEAD_DIM] for g in range(GROUP)], axis=0).T
    qb_t = (q_t * SCALE).astype(BF16)
    s = (jnp.dot(kb_ref[pl.ds(kstart, span), :], qb_t, preferred_element_type=F32)
         + bias_ref[jnp.minimum(i, 1)])
    sink = sink_ref[...]
    m = jnp.maximum(jnp.max(s, axis=0, keepdims=True), sink)
    p = jnp.exp(s - m)
    l = jnp.sum(p, axis=0, keepdims=True) + jnp.exp(sink - m)
    pb = p.astype(BF16)
    out_t = None
    for c in range(n_chunks):
        part = jnp.dot(vt_ref[first_chunk + c], pb[c * WINDOW:(c + 1) * WINDOW, :],
                       preferred_element_type=F32)
        out_t = part if out_t is None else out_t + part
    out_t = out_t / l
    for g in range(GROUP):
        os_ref[:, g * HEAD_DIM:(g + 1) * HEAD_DIM] = (
            out_t[:, g * tq:(g + 1) * tq].T.astype(os_ref.dtype))

    _memory_attention_tile(qm_ref, mk_ref, mv_ref, om_ref)


def prompt_attention(qkv, memkv, batch, seq, sinks_layer):
    nq = seq // Q_TILE
    k_col = Q_W // HEAD_DIM
    v_col = (Q_W + KV_W) // HEAD_DIM
    qm_col = (Q_W + 2 * KV_W) // HEAD_DIM
    smem = pl.BlockSpec(memory_space=pltpu.SMEM)
    tensor_specs = [
        pl.BlockSpec((Q_TILE, GROUP * HEAD_DIM), lambda b, h, i: (b * nq + i, h)),
        pl.BlockSpec((seq, HEAD_DIM), lambda b, h, i: (b, k_col + h)),
        pl.BlockSpec((seq, HEAD_DIM), lambda b, h, i: (b, v_col + h)),
        pl.BlockSpec((Q_TILE, HEAD_DIM), lambda b, h, i: (b * nq + i, qm_col + h)),
        pl.BlockSpec((N_MEM, HEAD_DIM), lambda b, h, i: (b, h)),
        pl.BlockSpec((N_MEM, HEAD_DIM), lambda b, h, i: (b, N_MEM_HEADS + h)),
    ]
    out_specs = [
        pl.BlockSpec((Q_TILE, GROUP * HEAD_DIM), lambda b, h, i: (b * nq + i, h)),
        pl.BlockSpec((Q_TILE, HEAD_DIM), lambda b, h, i: (b * nq + i, h)),
    ]
    out_shape = [jax.ShapeDtypeStruct((batch * seq, Q_W), BF16),
                 jax.ShapeDtypeStruct((batch * seq, MQ_W), BF16)]
    width = GROUP * Q_TILE
    row_scratch = pltpu.VMEM((1, width), F32)
    slopes = jnp.asarray(SLOPES)
    tensors = (qkv, qkv, qkv, qkv, memkv, memkv)
    if sinks_layer is None:
        return pl.pallas_call(
            _moba_prompt_kernel,
            grid=(batch, N_KV_HEADS, nq),
            in_specs=[smem] + tensor_specs,
            out_specs=out_specs, out_shape=out_shape,
            scratch_shapes=[
                pltpu.VMEM((seq, HEAD_DIM), BF16),
                pltpu.VMEM((seq // MOBA_BLOCK, HEAD_DIM, MOBA_BLOCK), BF16),
                pltpu.VMEM((seq // MOBA_BLOCK, HEAD_DIM), F32),
                pltpu.VMEM((MOBA_BLOCK, width), F32),
                row_scratch,
                pltpu.VMEM((seq // MOBA_BLOCK, width), F32),
                pltpu.VMEM((MOBA_BLOCK, width), F32),
                pltpu.VMEM((MOBA_BLOCK, width), F32),
                pltpu.VMEM((HEAD_DIM, width), F32),
                row_scratch,
                row_scratch,
            ],
            compiler_params=_params(3),
            name="moba_prompt",
        )(slopes, *tensors)
    return pl.pallas_call(
        _swa_prompt_kernel,
        grid=(batch, N_KV_HEADS, nq),
        in_specs=[smem, smem] + tensor_specs,
        out_specs=out_specs, out_shape=out_shape,
        scratch_shapes=[
            pltpu.VMEM((seq, HEAD_DIM), BF16),
            pltpu.VMEM((seq // WINDOW, HEAD_DIM, WINDOW), BF16),
            pltpu.VMEM((2, Q_TILE + WINDOW, width), F32),
            row_scratch,
        ],
        compiler_params=_params(3),
        name="swa_prompt",
    )(slopes, sinks_layer, *tensors)


def _export_kv_kernel(k0_ref, v0_ref, k1_ref, v1_ref, ok_ref, ov_ref):
    def export(k_ref, v_ref):
        for h in range(N_KV_HEADS):
            lanes = slice(h * HEAD_DIM, (h + 1) * HEAD_DIM)
            rows = pl.ds(h, k_ref.shape[0], stride=N_KV_HEADS)
            ok_ref[rows, :] = k_ref[:, lanes]
            ov_ref[rows, :] = v_ref[:, lanes]

    @pl.when(pl.program_id(0) == 0)
    def _():
        export(k0_ref, v0_ref)

    @pl.when(pl.program_id(0) == 1)
    def _():
        export(k1_ref, v1_ref)


def export_prompt_kv(qkv_pair, m, tm):
    k_col = Q_W // KV_W
    v_col = k_col + 1

    def spec(layer, col):
        return pl.BlockSpec((tm, KV_W), lambda a, i: (jnp.where(a == layer, i, 0), col))

    out_spec = pl.BlockSpec((None, tm * N_KV_HEADS, HEAD_DIM), lambda a, i: (a, i, 0))
    out = jax.ShapeDtypeStruct((2, m * N_KV_HEADS, HEAD_DIM), F32)
    q0, q1 = qkv_pair
    return pl.pallas_call(
        _export_kv_kernel,
        grid=(2, m // tm),
        in_specs=[spec(0, k_col), spec(0, v_col), spec(1, k_col), spec(1, v_col)],
        out_specs=[out_spec, out_spec], out_shape=[out, out],
        compiler_params=_params(2),
        name="export_prompt_kv",
    )(q0, q0, q1, q1)


def _head_rows(ref, head, n_tokens, n_heads):
    return ref[pl.ds(head, n_tokens, stride=n_heads), :]


def _row_scores(k, q_row):
    return jnp.sum(k * q_row, axis=-1, keepdims=True) * SCALE


def _sample_memory_kernel(qm_ref, mk_ref, mv_ref, o_ref):
    for h in range(N_MEM_HEADS):
        lanes = slice(h * HEAD_DIM, (h + 1) * HEAD_DIM)
        s = _row_scores(_head_rows(mk_ref, h, N_MEM, N_MEM_HEADS), qm_ref[:, lanes])
        p = jnp.exp(s - jnp.max(s, axis=0, keepdims=True))
        l = jnp.sum(p, axis=0, keepdims=True)
        v = _head_rows(mv_ref, h, N_MEM, N_MEM_HEADS)
        o_ref[:, lanes] = jnp.sum(p * v, axis=0, keepdims=True) / l


def sample_memory_attention(qm, mem_k, mem_v, layer):
    b = qm.shape[0]
    row = pl.BlockSpec((None, 1, MQ_W), lambda i: (i, 0, 0))
    cache = pl.BlockSpec((None, None, N_MEM * N_MEM_HEADS, HEAD_DIM), lambda i: (layer, i, 0, 0))
    return pl.pallas_call(
        _sample_memory_kernel,
        grid=(b,),
        in_specs=[row, cache, cache],
        out_specs=row,
        out_shape=jax.ShapeDtypeStruct((b, 1, MQ_W), F32),
        compiler_params=_params(1),
        name="sample_memory",
    )(qm, mem_k, mem_v)


def _sample_swa_kernel(sinks_ref, q_ref, kc_ref, vc_ref, kn_ref, vn_ref, o_ref):
    back = (WINDOW - lax.broadcasted_iota(jnp.int32, (WINDOW, 1), 0)).astype(F32)
    for hq in range(N_HEADS):
        kv = hq // GROUP
        q_row = q_ref[:, hq * HEAD_DIM:(hq + 1) * HEAD_DIM]
        lanes = slice(kv * HEAD_DIM, (kv + 1) * HEAD_DIM)
        sink = sinks_ref[hq]
        s_c = (_row_scores(_head_rows(kc_ref, kv, WINDOW, N_KV_HEADS), q_row)
               - float(SLOPES[hq]) * back)
        s_n = _row_scores(kn_ref[:, lanes], q_row)
        m = jnp.maximum(jnp.maximum(jnp.max(s_c, axis=0, keepdims=True), s_n), sink)
        p_c = jnp.exp(s_c - m)
        p_n = jnp.exp(s_n - m)
        l = jnp.sum(p_c, axis=0, keepdims=True) + p_n + jnp.exp(sink - m)
        v_c = _head_rows(vc_ref, kv, WINDOW, N_KV_HEADS)
        o = jnp.sum(p_c * v_c, axis=0, keepdims=True) + p_n * vn_ref[:, lanes]
        o_ref[:, hq * HEAD_DIM:(hq + 1) * HEAD_DIM] = o / l


def sample_swa_attention(q, k_new, v_new, cache_k, cache_v, sinks_layer, layer_b):
    b = q.shape[0]
    q_spec = pl.BlockSpec((None, 1, Q_W), lambda i: (i, 0, 0))
    kv_row = pl.BlockSpec((None, 1, KV_W), lambda i: (i, 0, 0))
    cache = pl.BlockSpec((None, None, WINDOW * N_KV_HEADS, HEAD_DIM),
                         lambda i: (layer_b, i, 0, 0))
    return pl.pallas_call(
        _sample_swa_kernel,
        grid=(b,),
        in_specs=[pl.BlockSpec(memory_space=pltpu.SMEM), q_spec, cache, cache, kv_row, kv_row],
        out_specs=q_spec,
        out_shape=jax.ShapeDtypeStruct((b, 1, Q_W), F32),
        compiler_params=_params(1),
        name="sample_swa",
    )(sinks_layer, q, cache_k, cache_v, k_new, v_new)


PAGE_ROWS = PAGE_SIZE * N_KV_HEADS
SUBLANES = 8


def _moba_gate_kernel(pt_ref, q_ref, *refs):
    del pt_ref
    pages = refs[:GATE_PAGES_PER_STEP]
    sel_ref, kmean_ref = refs[GATE_PAGES_PER_STEP:]
    s = pl.program_id(1)
    blocks_per_step = GATE_PAGES_PER_STEP // PAGES_PER_BLOCK
    n_blocks = kmean_ref.shape[1]

    sub = lax.broadcasted_iota(jnp.int32, (blocks_per_step, HEAD_DIM), 0)
    means = [jnp.zeros((blocks_per_step, HEAD_DIM), F32) for _ in range(N_KV_HEADS)]
    for r in range(blocks_per_step):
        total = None
        for page in pages[r * PAGES_PER_BLOCK:(r + 1) * PAGES_PER_BLOCK]:
            part = jnp.sum(page[...].reshape(PAGE_ROWS // SUBLANES, SUBLANES, HEAD_DIM), axis=0)
            total = part if total is None else total + part
        for kv in range(N_KV_HEADS):
            head_sum = total[kv:kv + 1, :] + total[kv + N_KV_HEADS:kv + N_KV_HEADS + 1, :]
            means[kv] = jnp.where(sub == r, head_sum * (1.0 / MOBA_BLOCK), means[kv])
    first = pl.multiple_of(s * blocks_per_step, blocks_per_step)
    for kv in range(N_KV_HEADS):
        kmean_ref[kv, pl.ds(first, blocks_per_step), :] = means[kv]

    @pl.when(s == pl.num_programs(1) - 1)
    def _():
        q = q_ref[...]
        head = lax.broadcasted_iota(jnp.int32, (N_HEADS, n_blocks), 0)
        gate = jnp.zeros((N_HEADS, n_blocks), F32)
        for kv in range(N_KV_HEADS):
            g_kv = _dot_t(q, kmean_ref[kv], precision=lax.Precision.HIGHEST)
            gate = jnp.where((head >= kv * GROUP) & (head < (kv + 1) * GROUP), g_kv, gate)
        blk = lax.broadcasted_iota(jnp.int32, (N_HEADS, n_blocks), 1).astype(F32)
        lane = lax.broadcasted_iota(jnp.int32, sel_ref.shape, 1)
        picks = jnp.zeros(sel_ref.shape, F32)
        for t in range(MOBA_TOPK):
            best = jnp.max(gate, axis=-1, keepdims=True)
            idx = jnp.min(jnp.where(gate == best, blk, float(n_blocks)), axis=-1, keepdims=True)
            picks = jnp.where(lane == t, idx, picks)
            gate = jnp.where(blk == idx, -jnp.inf, gate)
        sel_ref[...] = picks.astype(jnp.int32)


def sample_moba_select(q, cache_k, page_table, layer_a):
    b = q.shape[0]
    n_pages = page_table.shape[1]
    n_blocks = n_pages // PAGES_PER_BLOCK
    steps = n_pages // GATE_PAGES_PER_STEP

    def page_spec(r):
        return pl.BlockSpec(
            (None, None, PAGE_ROWS, HEAD_DIM),
            lambda i, s, pt: (layer_a, pt[i, s * GATE_PAGES_PER_STEP + r], 0, 0))

    grid_spec = pltpu.PrefetchScalarGridSpec(
        num_scalar_prefetch=1,
        grid=(b, steps),
        in_specs=[pl.BlockSpec((None, N_HEADS, HEAD_DIM), lambda i, s, pt: (i, 0, 0))]
        + [page_spec(r) for r in range(GATE_PAGES_PER_STEP)],
        out_specs=pl.BlockSpec((None, N_HEADS, HEAD_DIM), lambda i, s, pt: (i, 0, 0)),
        scratch_shapes=[pltpu.VMEM((N_KV_HEADS, n_blocks, HEAD_DIM), F32)],
    )
    sel = pl.pallas_call(
        _moba_gate_kernel,
        grid_spec=grid_spec,
        out_shape=jax.ShapeDtypeStruct((b, N_HEADS, HEAD_DIM), jnp.int32),
        compiler_params=_params(2),
        name="sample_moba_select",
    )(page_table, q, *([cache_k] * GATE_PAGES_PER_STEP))
    return sel[:, :, :MOBA_TOPK]


N_SEL_PAGES = GROUP * MOBA_TOPK * PAGES_PER_BLOCK


def _moba_sample_kernel(pt_ref, sel_ref, slopes_ref, q_ref, kn_ref, vn_ref, *refs, past_len):
    del pt_ref
    k_pages = refs[:N_SEL_PAGES]
    v_pages = refs[N_SEL_PAGES:2 * N_SEL_PAGES]
    o_ref = refs[2 * N_SEL_PAGES]
    b = pl.program_id(0)
    kvh = pl.program_id(1)
    within = lax.broadcasted_iota(jnp.int32, (PAGE_SIZE, 1), 0)
    for g in range(GROUP):
        head = kvh * GROUP + g
        slope = slopes_ref[head]
        q_row = q_ref[g:g + 1, :]
        s_own = _row_scores(kn_ref[...], q_row)
        scores = []
        m = s_own
        for t in range(MOBA_TOPK):
            first = sel_ref[(b * N_HEADS + head) * MOBA_TOPK + t] * MOBA_BLOCK
            for r in range(PAGES_PER_BLOCK):
                page = k_pages[(g * MOBA_TOPK + t) * PAGES_PER_BLOCK + r]
                dist = (past_len - (first + r * PAGE_SIZE) - within).astype(F32)
                s = _row_scores(_head_rows(page, kvh, PAGE_SIZE, N_KV_HEADS), q_row) - slope * dist
                scores.append(s)
                m = jnp.maximum(m, jnp.max(s, axis=0, keepdims=True))
        p_own = jnp.exp(s_own - m)
        l = p_own
        o = p_own * vn_ref[...]
        for idx, s in enumerate(scores):
            p = jnp.exp(s - m)
            l = l + jnp.sum(p, axis=0, keepdims=True)
            v = _head_rows(v_pages[g * MOBA_TOPK * PAGES_PER_BLOCK + idx], kvh, PAGE_SIZE, N_KV_HEADS)
            o = o + jnp.sum(p * v, axis=0, keepdims=True)
        o_ref[g:g + 1, :] = o / l


def sample_moba_attention(q, k_new, v_new, cache_k, cache_v, page_table, sel, layer_a):
    b = q.shape[0]
    past_len = page_table.shape[1] * PAGE_SIZE

    def page_spec(g, t, r):
        def index_map(i, h, pt, sl):
            block = sl[(i * N_HEADS + h * GROUP + g) * MOBA_TOPK + t]
            return (layer_a, pt[i, block * PAGES_PER_BLOCK + r], 0, 0)
        return pl.BlockSpec((None, None, PAGE_ROWS, HEAD_DIM), index_map)

    page_specs = [page_spec(g, t, r) for g in range(GROUP) for t in range(MOBA_TOPK)
                  for r in range(PAGES_PER_BLOCK)]
    q_spec = pl.BlockSpec((None, None, GROUP, HEAD_DIM), lambda i, h, pt, sl: (i, h, 0, 0))
    row_spec = pl.BlockSpec((None, None, 1, HEAD_DIM), lambda i, h, pt, sl: (i, h, 0, 0))
    grid_spec = pltpu.PrefetchScalarGridSpec(
        num_scalar_prefetch=2,
        grid=(b, N_KV_HEADS),
        in_specs=[pl.BlockSpec(memory_space=pltpu.SMEM), q_spec, row_spec, row_spec]
        + page_specs + page_specs,
        out_specs=q_spec,
    )
    return pl.pallas_call(
        functools.partial(_moba_sample_kernel, past_len=past_len),
        grid_spec=grid_spec,
        out_shape=jax.ShapeDtypeStruct((b, N_KV_HEADS, GROUP, HEAD_DIM), F32),
        compiler_params=_params(2),
        name="sample_moba",
    )(page_table, sel.reshape(-1), jnp.asarray(SLOPES), q, k_new, v_new,
      *([cache_k] * N_SEL_PAGES), *([cache_v] * N_SEL_PAGES))


DENSE_TM = 1024
DOWN_TM = 512


def _half_ffn(x, xg, ssq, gain, wg_all, wu_all, wd_all, layer, next_gain):
    a, a_tail = ffn_up(xg, ssq, x, gain, wg_all, wu_all, layer, tm=DENSE_TM, tn=512)
    return matmul_residual([a], [a_tail], wd_all, layer, x, 0.5, tm=DOWN_TM, tn=512,
                           next_gain=next_gain)


def kernel(x_prompt, x_sample, cache_moba_k, cache_moba_v, cache_swa_k, cache_swa_v, cache_mem_k, cache_mem_v, page_table, mem_prompt, g_ffn1, w_ffn1_gate, w_ffn1_up, w_ffn1_down, g_attn, w_in, w_out, sinks, g_mem, w_mem_kv, g_ffn2, w_ffn2_gate, w_ffn2_up, w_ffn2_down, g_final):
    bp, seq, d = x_prompt.shape
    bs = x_sample.shape[0]
    m = bp * seq
    n_pool = cache_moba_k.shape[1]
    mem = mem_prompt.reshape(bp * N_MEM, d)
    mem_k_cache = cache_mem_k.reshape(DEPTH, bs, N_MEM * N_MEM_HEADS, HEAD_DIM)
    mem_v_cache = cache_mem_v.reshape(DEPTH, bs, N_MEM * N_MEM_HEADS, HEAD_DIM)
    swa_k_cache = cache_swa_k.reshape(-1, bs, WINDOW * N_KV_HEADS, HEAD_DIM)
    swa_v_cache = cache_swa_v.reshape(-1, bs, WINDOW * N_KV_HEADS, HEAD_DIM)
    moba_k_flat = cache_moba_k.reshape(-1, n_pool, PAGE_ROWS, HEAD_DIM)
    moba_v_flat = cache_moba_v.reshape(-1, n_pool, PAGE_ROWS, HEAD_DIM)

    moba_qkv, moba_ks, moba_vs = [], [], []
    swa_kp, swa_vp, swa_ks, swa_vs = [], [], [], []
    mem_kp, mem_vp = [], []
    x, xg, ssq = stream_start(x_prompt.reshape(m, d), x_sample.reshape(bs, d), g_ffn1, 0,
                              tm=DOWN_TM)
    for l in range(DEPTH):
        x, xg, ssq = _half_ffn(x, xg, ssq, (g_ffn1, l), w_ffn1_gate, w_ffn1_up, w_ffn1_down, l,
                               (g_attn, l))

        qkv = norm_project(xg, ssq, x, (g_attn, l), w_in, l, tm=DENSE_TM, tn=1024)
        qkv_s = qkv[m:]
        memkv = project(rms_norm(mem, g_mem, l, BF16), w_mem_kv, l, tn=512)
        mem_kp.append(memkv[:, :MQ_W].reshape(bp, N_MEM, N_MEM_HEADS, HEAD_DIM))
        mem_vp.append(memkv[:, MQ_W:].reshape(bp, N_MEM, N_MEM_HEADS, HEAD_DIM))

        qs = qkv_s[:, :Q_W]
        ks = qkv_s[:, Q_W:Q_W + KV_W]
        vs = qkv_s[:, Q_W + KV_W:Q_W + 2 * KV_W]
        qms = qkv_s[:, Q_W + 2 * KV_W:]
        ks4 = ks.reshape(bs, 1, N_KV_HEADS, HEAD_DIM)
        vs4 = vs.reshape(bs, 1, N_KV_HEADS, HEAD_DIM)

        oms = sample_memory_attention(qms.reshape(bs, 1, MQ_W), mem_k_cache, mem_v_cache, l)
        j = l // 2
        if l % 2 == 0:
            osp, omp = prompt_attention(qkv, memkv, bp, seq, None)
            sel = sample_moba_select(qs.reshape(bs, N_HEADS, HEAD_DIM), moba_k_flat, page_table, j)
            oss = sample_moba_attention(
                qs.reshape(bs, N_KV_HEADS, GROUP, HEAD_DIM),
                ks.reshape(bs, N_KV_HEADS, 1, HEAD_DIM), vs.reshape(bs, N_KV_HEADS, 1, HEAD_DIM),
                moba_k_flat, moba_v_flat, page_table, sel, j)
            moba_qkv.append(qkv)
            moba_ks.append(ks4)
            moba_vs.append(vs4)
        else:
            osp, omp = prompt_attention(qkv, memkv, bp, seq, sinks[j])
            oss = sample_swa_attention(qs.reshape(bs, 1, Q_W), ks.reshape(bs, 1, KV_W),
                                       vs.reshape(bs, 1, KV_W), swa_k_cache, swa_v_cache,
                                       sinks[j], j)
            last = jnp.stack([qkv[(b + 1) * seq - WINDOW:(b + 1) * seq, Q_W:Q_W + 2 * KV_W]
                              for b in range(bp)])
            swa_kp.append(last[:, :, :KV_W].reshape(bp, WINDOW, N_KV_HEADS, HEAD_DIM))
            swa_vp.append(last[:, :, KV_W:].reshape(bp, WINDOW, N_KV_HEADS, HEAD_DIM))
            swa_ks.append(jnp.concatenate([cache_swa_k[j], ks4], axis=1)[:, -WINDOW:])
            swa_vs.append(jnp.concatenate([cache_swa_v[j], vs4], axis=1)[:, -WINDOW:])

        x, xg, ssq = matmul_residual(
            [osp, omp], [oss.reshape(bs, Q_W), oms.reshape(bs, MQ_W)], w_out, l, x, 1.0,
            tm=DENSE_TM, tn=1024, next_gain=(g_ffn2, l))

        next_gain = (g_ffn1, l + 1) if l + 1 < DEPTH else None
        out = _half_ffn(x, xg, ssq, (g_ffn2, l), w_ffn2_gate, w_ffn2_up, w_ffn2_down, l,
                        next_gain)
        if next_gain is None:
            (x,) = out
        else:
            x, xg, ssq = out

    y_prompt, y_sample = stream_end(x, g_final, m, tm=DOWN_TM)
    moba_kp, moba_vp = export_prompt_kv(moba_qkv, m, tm=DOWN_TM)
    kv_shape = (len(moba_qkv), bp, seq, N_KV_HEADS, HEAD_DIM)
    return (y_prompt.reshape(bp, seq, d), y_sample.reshape(bs, 1, d),
            moba_kp.reshape(kv_shape), moba_vp.reshape(kv_shape),
            jnp.stack(moba_ks), jnp.stack(moba_vs),
            jnp.stack(swa_kp), jnp.stack(swa_vp), jnp.stack(swa_ks), jnp.stack(swa_vs),
            jnp.stack(mem_kp), jnp.stack(mem_vp))
```

```python
import functools
import math

import jax
import jax.numpy as jnp
import numpy as np
from jax import lax
from jax.experimental import pallas as pl
from jax.experimental.pallas import tpu as pltpu

F32 = jnp.float32
BF16 = jnp.bfloat16

D_MODEL = 2048
DEPTH = 4
PAGE_SIZE = 128
HEAD_DIM = 128
N_HEADS = 12
N_KV_HEADS = 4
GROUP = N_HEADS // N_KV_HEADS
N_MEM_HEADS = 4
N_MEM = 256
D_FF = 5632
MOBA_BLOCK = 256
MOBA_TOPK = 3
WINDOW = 128
RMS_EPS = 1e-6
Q_W = N_HEADS * HEAD_DIM
KV_W = N_KV_HEADS * HEAD_DIM
MQ_W = N_MEM_HEADS * HEAD_DIM
IN_W = Q_W + 2 * KV_W + MQ_W
SCALE = HEAD_DIM ** -0.5
PAGES_PER_BLOCK = MOBA_BLOCK // PAGE_SIZE

VMEM_LIMIT_BYTES = 56 * 1024 * 1024
MASKED = -1e30
Q_TILE = 256
GATE_PAGES_PER_STEP = 32


def _alibi_slope_list(n):
    def pow2(m):
        start = 2.0 ** (-(2.0 ** -(math.log2(m) - 3)))
        return [start ** (i + 1) for i in range(m)]
    if math.log2(n).is_integer():
        return pow2(n)
    c = 2 ** math.floor(math.log2(n))
    return pow2(c) + _alibi_slope_list(2 * c)[0::2][:n - c]


SLOPES = np.array(_alibi_slope_list(N_HEADS), dtype=np.float32)


def _params(n_grid_dims):
    return pltpu.CompilerParams(
        dimension_semantics=("arbitrary",) * n_grid_dims,
        vmem_limit_bytes=VMEM_LIMIT_BYTES)


def _dot_t(a, b, precision=None):
    return lax.dot_general(a, b, (((1,), (1,)), ((), ())), precision=precision,
                           preferred_element_type=F32)


def _norm_kernel(x_ref, g_ref, o_ref):
    x = x_ref[...]
    ms = jnp.mean(x * x, axis=-1, keepdims=True)
    o_ref[...] = (x * lax.rsqrt(ms + RMS_EPS) * g_ref[...]).astype(o_ref.dtype)


def rms_norm(x, g_all, layer, out_dtype):
    m, d = x.shape
    tm = min(m, 512)
    g3 = g_all.reshape(g_all.shape[0], 1, d)
    return pl.pallas_call(
        _norm_kernel,
        grid=(m // tm,),
        in_specs=[pl.BlockSpec((tm, d), lambda i: (i, 0)),
                  pl.BlockSpec((None, 1, d), lambda i: (layer, 0, 0))],
        out_specs=pl.BlockSpec((tm, d), lambda i: (i, 0)),
        out_shape=jax.ShapeDtypeStruct((m, d), out_dtype),
        compiler_params=_params(1),
        name="rms_norm",
    )(x, g3)


def _proj_kernel(h_ref, w_ref, o_ref, wb_ref):
    @pl.when(pl.program_id(1) == 0)
    def _():
        wb_ref[...] = w_ref[...].astype(BF16)
    o_ref[...] = jnp.dot(h_ref[...], wb_ref[...], preferred_element_type=F32)


def project(h, w_all, layer, tn):
    m, k = h.shape
    n = w_all.shape[2]
    tm = min(m, 1024)
    return pl.pallas_call(
        _proj_kernel,
        grid=(n // tn, m // tm),
        in_specs=[pl.BlockSpec((tm, k), lambda j, i: (i, 0)),
                  pl.BlockSpec((None, k, tn), lambda j, i: (layer, 0, j))],
        out_specs=pl.BlockSpec((tm, tn), lambda j, i: (i, j)),
        out_shape=jax.ShapeDtypeStruct((m, n), F32),
        scratch_shapes=[pltpu.VMEM((k, tn), BF16)],
        compiler_params=_params(2),
        name="project",
    )(h, w_all)


LANES = 128


def _tail_first(i, n_full):
    return jnp.where(i == 0, n_full, i - 1)


def _main_tile(i):
    return jnp.maximum(i - 1, 0)


def _lane_partial_sums(x):
    total = x[:, :LANES]
    for c in range(1, x.shape[1] // LANES):
        total = total + x[:, c * LANES:(c + 1) * LANES]
    return total


def _store_norm_inputs(x, g_ref, xg_ref, ssq_ref):
    xg_ref[...] = (x * g_ref[...]).astype(xg_ref.dtype)
    ssq_ref[...] = _lane_partial_sums(x * x)


def _inv_rms(ssq, d_model):
    total = jnp.sum(jnp.sum(ssq, axis=0), axis=-1, keepdims=True)
    return lax.rsqrt(total * (1.0 / d_model) + RMS_EPS)


def _normed_rows(x, g):
    return x * lax.rsqrt(jnp.mean(x * x, axis=-1, keepdims=True) + RMS_EPS) * g


def _dot_split(a, w, w_hi):
    a_hi = a.astype(BF16)
    a_lo = (a - a_hi.astype(F32)).astype(BF16)
    w_lo = (w - w_hi.astype(F32)).astype(BF16)
    t = a.shape[0]
    on_hi = jnp.dot(jnp.concatenate([a_hi, a_lo], axis=0), w_hi, preferred_element_type=F32)
    return on_hi[:t] + on_hi[t:] + jnp.dot(a_hi, w_lo, preferred_element_type=F32)


def _stream_start_kernel(xp_ref, xs_ref, g_ref, x_ref, xg_ref, ssq_ref, *, tail):
    i = pl.program_id(0)

    @pl.when(i == 0)
    def _():
        x_ref[0:tail, :] = xs_ref[...]

    @pl.when(i > 0)
    def _():
        x = xp_ref[...]
        x_ref[...] = x
        _store_norm_inputs(x, g_ref, xg_ref, ssq_ref)


def stream_start(x_main, x_tail, g_all, layer, tm):
    m, d = x_main.shape
    tail = x_tail.shape[0]
    n_full = m // tm
    g3 = g_all.reshape(g_all.shape[0], 1, d)
    main_block = lambda i: (_main_tile(i), 0)
    return pl.pallas_call(
        functools.partial(_stream_start_kernel, tail=tail),
        grid=(n_full + 1,),
        in_specs=[pl.BlockSpec((tm, d), main_block),
                  pl.BlockSpec((tail, d), lambda i: (0, 0)),
                  pl.BlockSpec((None, 1, d), lambda i: (layer, 0, 0))],
        out_specs=[pl.BlockSpec((tm, d), lambda i: (_tail_first(i, n_full), 0)),
                   pl.BlockSpec((tm, d), main_block),
                   pl.BlockSpec((None, tm, LANES), lambda i: (0, _main_tile(i), 0))],
        out_shape=[jax.ShapeDtypeStruct((m + tail, d), F32), jax.ShapeDtypeStruct((m, d), BF16),
                   jax.ShapeDtypeStruct((1, m, LANES), F32)],
        compiler_params=_params(1),
        name="stream_start",
    )(x_main, x_tail, g3)


def _stream_end_kernel(x_ref, g_ref, yp_ref, ys_ref, *, tail):
    i = pl.program_id(0)

    @pl.when(i == 0)
    def _():
        ys_ref[...] = _normed_rows(x_ref[0:tail, :], g_ref[...])

    @pl.when(i > 0)
    def _():
        yp_ref[...] = _normed_rows(x_ref[...], g_ref[...])


def stream_end(x, g, m, tm):
    rows, d = x.shape
    tail = rows - m
    n_full = m // tm
    return pl.pallas_call(
        functools.partial(_stream_end_kernel, tail=tail),
        grid=(n_full + 1,),
        in_specs=[pl.BlockSpec((tm, d), lambda i: (_tail_first(i, n_full), 0)),
                  pl.BlockSpec((1, d), lambda i: (0, 0))],
        out_specs=[pl.BlockSpec((tm, d), lambda i: (_main_tile(i), 0)),
                   pl.BlockSpec((tail, d), lambda i: (0, 0))],
        out_shape=[jax.ShapeDtypeStruct((m, d), F32), jax.ShapeDtypeStruct((tail, d), F32)],
        compiler_params=_params(1),
        name="stream_end",
    )(x, g.reshape(1, d))


def _normed_specs(x, gain, nj, tm, n_full):
    rows, k = x.shape
    tail = rows - n_full * tm
    g_all, g_layer = gain
    specs = [pl.BlockSpec((tm, k), lambda j, i: (_main_tile(i), 0)),
             pl.BlockSpec((nj, tm, LANES),
                          lambda j, i: (0, jnp.where(j == 0, _main_tile(i), n_full - 1), 0)),
             pl.BlockSpec((tail, k), lambda j, i: (n_full * tm // tail, 0)),
             pl.BlockSpec((None, 1, k), lambda j, i: (g_layer, 0, 0))]
    return specs, [x, g_all.reshape(-1, 1, k)]


def _cached_inv_rms(ssq_ref, inv_ref, j, i, d_model):
    @pl.when(j == 0)
    def _():
        inv_ref[i] = _inv_rms(ssq_ref[...], d_model)
    return inv_ref[i]


def _norm_proj_kernel(xg_ref, ssq_ref, xt_ref, g_ref, w_ref, o_ref, wb_ref, inv_ref, *, tail):
    j = pl.program_id(0)
    i = pl.program_id(1)

    @pl.when(i == 0)
    def _():
        wb_ref[...] = w_ref[...].astype(BF16)
        h = _normed_rows(xt_ref[...], g_ref[...])
        o_ref[0:tail, :] = _dot_split(h, w_ref[...], wb_ref[...])

    @pl.when(i > 0)
    def _():
        inv = _cached_inv_rms(ssq_ref, inv_ref, j, i, xg_ref.shape[1])
        o_ref[...] = inv * jnp.dot(xg_ref[...], wb_ref[...], preferred_element_type=F32)


def norm_project(xg, ssq, x, gain, w_all, layer, tm, tn):
    m, k = xg.shape
    rows = x.shape[0]
    n = w_all.shape[2]
    n_full = m // tm
    specs, operands = _normed_specs(x, gain, ssq.shape[0], tm, n_full)
    return pl.pallas_call(
        functools.partial(_norm_proj_kernel, tail=rows - m),
        grid=(n // tn, n_full + 1),
        in_specs=specs + [pl.BlockSpec((None, k, tn), lambda j, i: (layer, 0, j))],
        out_specs=pl.BlockSpec((tm, tn), lambda j, i: (_tail_first(i, n_full), j)),
        out_shape=jax.ShapeDtypeStruct((rows, n), F32),
        scratch_shapes=[pltpu.VMEM((k, tn), BF16), pltpu.VMEM((n_full + 1, tm, 1), F32)],
        compiler_params=_params(2),
        name="norm_project",
    )(xg, ssq, *operands, w_all)


def _ffn_up_kernel(xg_ref, ssq_ref, xt_ref, g_ref, wg_ref, wu_ref, a_ref, at_ref,
                   wgb_ref, wub_ref, inv_ref):
    j = pl.program_id(0)
    i = pl.program_id(1)

    def swiglu(g, u):
        return g * jax.nn.sigmoid(g) * u

    @pl.when(i == 0)
    def _():
        wgb_ref[...] = wg_ref[...].astype(BF16)
        wub_ref[...] = wu_ref[...].astype(BF16)
        h = _normed_rows(xt_ref[...], g_ref[...])
        at_ref[...] = swiglu(_dot_split(h, wg_ref[...], wgb_ref[...]),
                             _dot_split(h, wu_ref[...], wub_ref[...]))

    @pl.when(i > 0)
    def _():
        inv = _cached_inv_rms(ssq_ref, inv_ref, j, i, xg_ref.shape[1])
        h = xg_ref[...]
        g = inv * jnp.dot(h, wgb_ref[...], preferred_element_type=F32)
        u = inv * jnp.dot(h, wub_ref[...], preferred_element_type=F32)
        a_ref[...] = swiglu(g, u).astype(a_ref.dtype)


def ffn_up(xg, ssq, x, gain, wg_all, wu_all, layer, tm, tn):
    m, k = xg.shape
    tail = x.shape[0] - m
    n = wg_all.shape[2]
    n_full = m // tm
    specs, operands = _normed_specs(x, gain, ssq.shape[0], tm, n_full)
    w_spec = pl.BlockSpec((None, k, tn), lambda j, i: (layer, 0, j))
    return pl.pallas_call(
        _ffn_up_kernel,
        grid=(n // tn, n_full + 1),
        in_specs=specs + [w_spec, w_spec],
        out_specs=[pl.BlockSpec((tm, tn), lambda j, i: (_main_tile(i), j)),
                   pl.BlockSpec((tail, tn), lambda j, i: (0, j))],
        out_shape=[jax.ShapeDtypeStruct((m, n), BF16), jax.ShapeDtypeStruct((tail, n), F32)],
        scratch_shapes=[pltpu.VMEM((k, tn), BF16), pltpu.VMEM((k, tn), BF16),
                        pltpu.VMEM((n_full + 1, tm, 1), F32)],
        compiler_params=_params(2),
        name="ffn_up",
    )(xg, ssq, *operands, wg_all, wu_all)


def _mm_res_kernel(*refs, k_sizes, emit_norm_inputs, scale, tail):
    n_a = len(k_sizes)
    main_refs = refs[:n_a]
    tail_refs = refs[n_a:2 * n_a]
    if emit_norm_inputs:
        w_ref, x_ref, g_ref, o_ref, xg_ref, ssq_ref, wb_ref = refs[2 * n_a:]
    else:
        w_ref, x_ref, o_ref, wb_ref = refs[2 * n_a:]
    i = pl.program_id(1)

    def product(a_refs, dot):
        acc = None
        off = 0
        for a_ref, ksz in zip(a_refs, k_sizes):
            d = dot(a_ref[...], slice(off, off + ksz))
            acc = d if acc is None else acc + d
            off += ksz
        return acc

    @pl.when(i == 0)
    def _():
        wb_ref[...] = w_ref[...].astype(BF16)
        acc = product(tail_refs, lambda a, rows: _dot_split(a, w_ref[rows, :], wb_ref[rows, :]))
        o_ref[0:tail, :] = x_ref[0:tail, :] + scale * acc

    @pl.when(i > 0)
    def _():
        acc = product(main_refs, lambda a, rows: jnp.dot(a.astype(BF16), wb_ref[rows, :],
                                                         preferred_element_type=F32))
        x = x_ref[...] + scale * acc
        o_ref[...] = x
        if emit_norm_inputs:
            _store_norm_inputs(x, g_ref, xg_ref, ssq_ref)


def matmul_residual(a_main, a_tail, w_all, layer, x, scale, tm, tn, next_gain=None):
    rows, n = x.shape
    m = a_main[0].shape[0]
    tail = rows - m
    n_full = m // tm
    k_sizes = tuple(a.shape[1] for a in a_main)
    k = sum(k_sizes)
    nj = n // tn
    a_specs = [pl.BlockSpec((tm, ks), lambda j, i: (_main_tile(i), 0)) for ks in k_sizes]
    a_specs += [pl.BlockSpec((tail, ks), lambda j, i: (0, 0)) for ks in k_sizes]
    tile = pl.BlockSpec((tm, tn), lambda j, i: (_tail_first(i, n_full), j))
    in_specs = a_specs + [pl.BlockSpec((None, k, tn), lambda j, i: (layer, 0, j)), tile]
    operands = list(a_main) + list(a_tail) + [w_all, x]
    out_specs = [tile]
    out_shape = [jax.ShapeDtypeStruct((rows, n), F32)]
    emit = next_gain is not None
    if emit:
        g_all, g_layer = next_gain
        in_specs.append(pl.BlockSpec((None, 1, tn), lambda j, i: (g_layer, 0, j)))
        operands.append(g_all.reshape(-1, 1, n))
        out_specs += [pl.BlockSpec((tm, tn), lambda j, i: (_main_tile(i), j)),
                      pl.BlockSpec((None, tm, LANES), lambda j, i: (j, _main_tile(i), 0))]
        out_shape += [jax.ShapeDtypeStruct((m, n), BF16),
                      jax.ShapeDtypeStruct((nj, m, LANES), F32)]
    return pl.pallas_call(
        functools.partial(_mm_res_kernel, k_sizes=k_sizes, emit_norm_inputs=emit, scale=scale,
                          tail=tail),
        grid=(nj, n_full + 1),
        in_specs=in_specs, out_specs=out_specs, out_shape=out_shape,
        scratch_shapes=[pltpu.VMEM((k, tn), BF16)],
        compiler_params=_params(2),
        name="matmul_residual",
    )(*operands)


def _softmax_pv(s, v):
    m = jnp.max(s, axis=-1, keepdims=True)
    p = jnp.exp(s - m)
    l = jnp.sum(p, axis=-1, keepdims=True)
    return jnp.dot(p.astype(BF16), v, preferred_element_type=F32), m, l


def _memory_attention_tile(qm_ref, mk_ref, mv_ref, om_ref):
    qm = (qm_ref[...] * SCALE).astype(BF16)
    s = _dot_t(qm, mk_ref[...].astype(BF16))
    o, _, l = _softmax_pv(s, mv_ref[...].astype(BF16))
    om_ref[...] = (o / l).astype(om_ref.dtype)


def _moba_prompt_kernel(slopes_ref, q_ref, k_ref, v_ref, qm_ref, mk_ref, mv_ref,
                        os_ref, om_ref, kb_ref, vt_ref, kmean_ref, bias_ref, nslope_ref,
                        chosen_ref, sa_ref, sb_ref, acc_ref, m_ref, l_ref):
    kvh = pl.program_id(1)
    i = pl.program_id(2)
    nb = kmean_ref.shape[0]
    tq = q_ref.shape[0]
    width = GROUP * tq
    key = lax.broadcasted_iota(jnp.int32, (MOBA_BLOCK, width), 0)
    query = lax.broadcasted_iota(jnp.int32, (MOBA_BLOCK, width), 1) & (tq - 1)

    @pl.when(i == 0)
    def _():
        kb_ref[...] = k_ref[...].astype(BF16)
        for n in range(nb):
            rows = slice(n * MOBA_BLOCK, (n + 1) * MOBA_BLOCK)
            vt_ref[n] = v_ref[rows, :].T.astype(BF16)
            kmean_ref[n:n + 1, :] = jnp.mean(k_ref[rows, :], axis=0, keepdims=True)
        lane = lax.broadcasted_iota(jnp.int32, (1, width), 1)
        nslope = jnp.zeros((1, width), F32)
        for g in range(GROUP):
            nslope = jnp.where((lane >= g * tq) & (lane < (g + 1) * tq),
                               -slopes_ref[kvh * GROUP + g], nslope)
        nslope_ref[...] = nslope
        bias_ref[...] = nslope * (query - key).astype(F32)

    q_t = jnp.concatenate(
        [q_ref[:, g * HEAD_DIM:(g + 1) * HEAD_DIM] for g in range(GROUP)], axis=0).T
    gate = jnp.dot(kmean_ref[...], q_t, precision=lax.Precision.HIGHEST,
                   preferred_element_type=F32)
    blk = lax.broadcasted_iota(jnp.int32, (nb, width), 0)
    rank = jnp.zeros((nb, width), jnp.int32)
    for m in range(nb - 1):
        gm = gate[m:m + 1, :]
        beats = (gm > gate) | ((gm == gate) & (m < blk))
        rank = rank + jnp.where(beats & (m < i), 1, 0)
    chosen_ref[...] = jnp.where((blk < i) & (rank < MOBA_TOPK), 1.0, 0.0)

    qb_t = (q_t * SCALE).astype(BF16)

    def scores(block):
        start = pl.multiple_of(block * MOBA_BLOCK, MOBA_BLOCK)
        return jnp.dot(kb_ref[pl.ds(start, MOBA_BLOCK), :], qb_t, preferred_element_type=F32)

    sa_ref[...] = scores(0)
    s = jnp.where(query >= key, scores(i) + bias_ref[...], MASKED)
    m_own = jnp.max(s, axis=0, keepdims=True)
    p = jnp.exp(s - m_own)
    m_ref[...] = m_own
    l_ref[...] = jnp.sum(p, axis=0, keepdims=True)
    acc_ref[...] = jnp.dot(vt_ref[i], p.astype(BF16), preferred_element_type=F32)

    def absorb(s_ref, block):
        offset = ((i - block) * MOBA_BLOCK).astype(F32)
        shift = jnp.where(chosen_ref[pl.ds(block, 1), :] > 0.0, nslope_ref[...] * offset, MASKED)
        s = s_ref[...] + bias_ref[...] + shift
        m_old = m_ref[...]
        m_new = jnp.maximum(m_old, jnp.max(s, axis=0, keepdims=True))
        alpha = jnp.exp(m_old - m_new)
        p = jnp.exp(s - m_new)
        m_ref[...] = m_new
        l_ref[...] = alpha * l_ref[...] + jnp.sum(p, axis=0, keepdims=True)
        acc_ref[...] = alpha * acc_ref[...] + jnp.dot(vt_ref[block], p.astype(BF16),
                                                      preferred_element_type=F32)

    def past_pair(t, carry):
        first = 2 * t
        second = jnp.minimum(first + 1, nb - 1)
        sb_ref[...] = scores(second)
        absorb(sa_ref, first)
        sa_ref[...] = scores(jnp.minimum(first + 2, nb - 1))
        absorb(sb_ref, second)
        return carry

    lax.fori_loop(0, (i + 1) // 2, past_pair, 0)
    out_t = acc_ref[...] / l_ref[...]
    for g in range(GROUP):
        os_ref[:, g * HEAD_DIM:(g + 1) * HEAD_DIM] = (
            out_t[:, g * tq:(g + 1) * tq].T.astype(os_ref.dtype))

    _memory_attention_tile(qm_ref, mk_ref, mv_ref, om_ref)


def _swa_prompt_kernel(slopes_ref, sinks_ref, q_ref, k_ref, v_ref, qm_ref, mk_ref, mv_ref,
                       os_ref, om_ref, kb_ref, vt_ref, bias_ref, sink_ref):
    kvh = pl.program_id(1)
    i = pl.program_id(2)
    tq = q_ref.shape[0]
    width = GROUP * tq
    n_chunks = tq // WINDOW + 1
    span = n_chunks * WINDOW

    @pl.when(i == 0)
    def _():
        kb_ref[...] = k_ref[...].astype(BF16)
        for n in range(vt_ref.shape[0]):
            vt_ref[n] = v_ref[n * WINDOW:(n + 1) * WINDOW, :].T.astype(BF16)
        lane = lax.broadcasted_iota(jnp.int32, (1, width), 1)
        nslope = jnp.zeros((1, width), F32)
        sink = jnp.zeros((1, width), F32)
        for g in range(GROUP):
            in_head = (lane >= g * tq) & (lane < (g + 1) * tq)
            nslope = jnp.where(in_head, -slopes_ref[kvh * GROUP + g], nslope)
            sink = jnp.where(in_head, sinks_ref[kvh * GROUP + g], sink)
        sink_ref[...] = sink
        key = lax.broadcasted_iota(jnp.int32, (span, width), 0)
        query = lax.broadcasted_iota(jnp.int32, (span, width), 1) & (tq - 1)
        for placement, lead in enumerate((0, WINDOW)):
            dist = query - key + lead
            bias_ref[placement] = jnp.where((dist >= 0) & (dist <= WINDOW),
                                            nslope * dist.astype(F32), MASKED)

    first_chunk = jnp.maximum(i * (tq // WINDOW) - 1, 0)
    kstart = pl.multiple_of(first_chunk * WINDOW, WINDOW)
    q_t = jnp.concatenate(
        [q_ref[:, g * HEAD_DIM:(g + 1) * HEAD_DIM] for g in range(GROUP)], axis=0).T
    qb_t = (q_t * SCALE).astype(BF16)
    s = (jnp.dot(kb_ref[pl.ds(kstart, span), :], qb_t, preferred_element_type=F32)
         + bias_ref[jnp.minimum(i, 1)])
    sink = sink_ref[...]
    m = jnp.maximum(jnp.max(s, axis=0, keepdims=True), sink)
    p = jnp.exp(s - m)
    l = jnp.sum(p, axis=0, keepdims=True) + jnp.exp(sink - m)
    pb = p.astype(BF16)
    out_t = None
    for c in range(n_chunks):
        part = jnp.dot(vt_ref[first_chunk + c], pb[c * WINDOW:(c + 1) * WINDOW, :],
                       preferred_element_type=F32)
        out_t = part if out_t is None else out_t + part
    out_t = out_t / l
    for g in range(GROUP):
        os_ref[:, g * HEAD_DIM:(g + 1) * HEAD_DIM] = (
            out_t[:, g * tq:(g + 1) * tq].T.astype(os_ref.dtype))

    _memory_attention_tile(qm_ref, mk_ref, mv_ref, om_ref)


def prompt_attention(qkv, memkv, batch, seq, sinks_layer):
    nq = seq // Q_TILE
    k_col = Q_W // HEAD_DIM
    v_col = (Q_W + KV_W) // HEAD_DIM
    qm_col = (Q_W + 2 * KV_W) // HEAD_DIM
    smem = pl.BlockSpec(memory_space=pltpu.SMEM)
    tensor_specs = [
        pl.BlockSpec((Q_TILE, GROUP * HEAD_DIM), lambda b, h, i: (b * nq + i, h)),
        pl.BlockSpec((seq, HEAD_DIM), lambda b, h, i: (b, k_col + h)),
        pl.BlockSpec((seq, HEAD_DIM), lambda b, h, i: (b, v_col + h)),
        pl.BlockSpec((Q_TILE, HEAD_DIM), lambda b, h, i: (b * nq + i, qm_col + h)),
        pl.BlockSpec((N_MEM, HEAD_DIM), lambda b, h, i: (b, h)),
        pl.BlockSpec((N_MEM, HEAD_DIM), lambda b, h, i: (b, N_MEM_HEADS + h)),
    ]
    out_specs = [
        pl.BlockSpec((Q_TILE, GROUP * HEAD_DIM), lambda b, h, i: (b * nq + i, h)),
        pl.BlockSpec((Q_TILE, HEAD_DIM), lambda b, h, i: (b * nq + i, h)),
    ]
    out_shape = [jax.ShapeDtypeStruct((batch * seq, Q_W), BF16),
                 jax.ShapeDtypeStruct((batch * seq, MQ_W), BF16)]
    width = GROUP * Q_TILE
    row_scratch = pltpu.VMEM((1, width), F32)
    slopes = jnp.asarray(SLOPES)
    tensors = (qkv, qkv, qkv, qkv, memkv, memkv)
    if sinks_layer is None:
        return pl.pallas_call(
            _moba_prompt_kernel,
            grid=(batch, N_KV_HEADS, nq),
            in_specs=[smem] + tensor_specs,
            out_specs=out_specs, out_shape=out_shape,
            scratch_shapes=[
                pltpu.VMEM((seq, HEAD_DIM), BF16),
                pltpu.VMEM((seq // MOBA_BLOCK, HEAD_DIM, MOBA_BLOCK), BF16),
                pltpu.VMEM((seq // MOBA_BLOCK, HEAD_DIM), F32),
                pltpu.VMEM((MOBA_BLOCK, width), F32),
                row_scratch,
                pltpu.VMEM((seq // MOBA_BLOCK, width), F32),
                pltpu.VMEM((MOBA_BLOCK, width), F32),
                pltpu.VMEM((MOBA_BLOCK, width), F32),
                pltpu.VMEM((HEAD_DIM, width), F32),
                row_scratch,
                row_scratch,
            ],
            compiler_params=_params(3),
            name="moba_prompt",
        )(slopes, *tensors)
    return pl.pallas_call(
        _swa_prompt_kernel,
        grid=(batch, N_KV_HEADS, nq),
        in_specs=[smem, smem] + tensor_specs,
        out_specs=out_specs, out_shape=out_shape,
        scratch_shapes=[
            pltpu.VMEM((seq, HEAD_DIM), BF16),
            pltpu.VMEM((seq // WINDOW, HEAD_DIM, WINDOW), BF16),
            pltpu.VMEM((2, Q_TILE + WINDOW, width), F32),
            row_scratch,
        ],
        compiler_params=_params(3),
        name="swa_prompt",
    )(slopes, sinks_layer, *tensors)


def _export_kv_kernel(k0_ref, v0_ref, k1_ref, v1_ref, ok_ref, ov_ref):
    def export(k_ref, v_ref):
        for h in range(N_KV_HEADS):
            lanes = slice(h * HEAD_DIM, (h + 1) * HEAD_DIM)
            rows = pl.ds(h, k_ref.shape[0], stride=N_KV_HEADS)
            ok_ref[rows, :] = k_ref[:, lanes]
            ov_ref[rows, :] = v_ref[:, lanes]

    @pl.when(pl.program_id(0) == 0)
    def _():
        export(k0_ref, v0_ref)

    @pl.when(pl.program_id(0) == 1)
    def _():
        export(k1_ref, v1_ref)


def export_prompt_kv(qkv_pair, m, tm):
    k_col = Q_W // KV_W
    v_col = k_col + 1

    def spec(layer, col):
        return pl.BlockSpec((tm, KV_W), lambda a, i: (jnp.where(a == layer, i, 0), col))

    out_spec = pl.BlockSpec((None, tm * N_KV_HEADS, HEAD_DIM), lambda a, i: (a, i, 0))
    out = jax.ShapeDtypeStruct((2, m * N_KV_HEADS, HEAD_DIM), F32)
    q0, q1 = qkv_pair
    return pl.pallas_call(
        _export_kv_kernel,
        grid=(2, m // tm),
        in_specs=[spec(0, k_col), spec(0, v_col), spec(1, k_col), spec(1, v_col)],
        out_specs=[out_spec, out_spec], out_shape=[out, out],
        compiler_params=_params(2),
        name="export_prompt_kv",
    )(q0, q0, q1, q1)


def _head_rows(ref, head, n_tokens, n_heads):
    return ref[pl.ds(head, n_tokens, stride=n_heads), :]


def _row_scores(k, q_row):
    return jnp.sum(k * q_row, axis=-1, keepdims=True) * SCALE


def _sample_memory_kernel(qm_ref, mk_ref, mv_ref, o_ref):
    for h in range(N_MEM_HEADS):
        lanes = slice(h * HEAD_DIM, (h + 1) * HEAD_DIM)
        s = _row_scores(_head_rows(mk_ref, h, N_MEM, N_MEM_HEADS), qm_ref[:, lanes])
        p = jnp.exp(s - jnp.max(s, axis=0, keepdims=True))
        l = jnp.sum(p, axis=0, keepdims=True)
        v = _head_rows(mv_ref, h, N_MEM, N_MEM_HEADS)
        o_ref[:, lanes] = jnp.sum(p * v, axis=0, keepdims=True) / l


def sample_memory_attention(qm, mem_k, mem_v, layer):
    b = qm.shape[0]
    row = pl.BlockSpec((None, 1, MQ_W), lambda i: (i, 0, 0))
    cache = pl.BlockSpec((None, None, N_MEM * N_MEM_HEADS, HEAD_DIM), lambda i: (layer, i, 0, 0))
    return pl.pallas_call(
        _sample_memory_kernel,
        grid=(b,),
        in_specs=[row, cache, cache],
        out_specs=row,
        out_shape=jax.ShapeDtypeStruct((b, 1, MQ_W), F32),
        compiler_params=_params(1),
        name="sample_memory",
    )(qm, mem_k, mem_v)


def _sample_swa_kernel(sinks_ref, q_ref, kc_ref, vc_ref, kn_ref, vn_ref, o_ref):
    back = (WINDOW - lax.broadcasted_iota(jnp.int32, (WINDOW, 1), 0)).astype(F32)
    for hq in range(N_HEADS):
        kv = hq // GROUP
        q_row = q_ref[:, hq * HEAD_DIM:(hq + 1) * HEAD_DIM]
        lanes = slice(kv * HEAD_DIM, (kv + 1) * HEAD_DIM)
        sink = sinks_ref[hq]
        s_c = (_row_scores(_head_rows(kc_ref, kv, WINDOW, N_KV_HEADS), q_row)
               - float(SLOPES[hq]) * back)
        s_n = _row_scores(kn_ref[:, lanes], q_row)
        m = jnp.maximum(jnp.maximum(jnp.max(s_c, axis=0, keepdims=True), s_n), sink)
        p_c = jnp.exp(s_c - m)
        p_n = jnp.exp(s_n - m)
        l = jnp.sum(p_c, axis=0, keepdims=True) + p_n + jnp.exp(sink - m)
        v_c = _head_rows(vc_ref, kv, WINDOW, N_KV_HEADS)
        o = jnp.sum(p_c * v_c, axis=0, keepdims=True) + p_n * vn_ref[:, lanes]
        o_ref[:, hq * HEAD_DIM:(hq + 1) * HEAD_DIM] = o / l


def sample_swa_attention(q, k_new, v_new, cache_k, cache_v, sinks_layer, layer_b):
    b = q.shape[0]
    q_spec = pl.BlockSpec((None, 1, Q_W), lambda i: (i, 0, 0))
    kv_row = pl.BlockSpec((None, 1, KV_W), lambda i: (i, 0, 0))
    cache = pl.BlockSpec((None, None, WINDOW * N_KV_HEADS, HEAD_DIM),
                         lambda i: (layer_b, i, 0, 0))
    return pl.pallas_call(
        _sample_swa_kernel,
        grid=(b,),
        in_specs=[pl.BlockSpec(memory_space=pltpu.SMEM), q_spec, cache, cache, kv_row, kv_row],
        out_specs=q_spec,
        out_shape=jax.ShapeDtypeStruct((b, 1, Q_W), F32),
        compiler_params=_params(1),
        name="sample_swa",
    )(sinks_layer, q, cache_k, cache_v, k_new, v_new)


PAGE_ROWS = PAGE_SIZE * N_KV_HEADS
SUBLANES = 8


def _moba_gate_kernel(pt_ref, q_ref, *refs):
    del pt_ref
    pages = refs[:GATE_PAGES_PER_STEP]
    sel_ref, kmean_ref = refs[GATE_PAGES_PER_STEP:]
    s = pl.program_id(1)
    blocks_per_step = GATE_PAGES_PER_STEP // PAGES_PER_BLOCK
    n_blocks = kmean_ref.shape[1]

    sub = lax.broadcasted_iota(jnp.int32, (blocks_per_step, HEAD_DIM), 0)
    means = [jnp.zeros((blocks_per_step, HEAD_DIM), F32) for _ in range(N_KV_HEADS)]
    for r in range(blocks_per_step):
        total = None
        for page in pages[r * PAGES_PER_BLOCK:(r + 1) * PAGES_PER_BLOCK]:
            part = jnp.sum(page[...].reshape(PAGE_ROWS // SUBLANES, SUBLANES, HEAD_DIM), axis=0)
            total = part if total is None else total + part
        for kv in range(N_KV_HEADS):
            head_sum = total[kv:kv + 1, :] + total[kv + N_KV_HEADS:kv + N_KV_HEADS + 1, :]
            means[kv] = jnp.where(sub == r, head_sum * (1.0 / MOBA_BLOCK), means[kv])
    first = pl.multiple_of(s * blocks_per_step, blocks_per_step)
    for kv in range(N_KV_HEADS):
        kmean_ref[kv, pl.ds(first, blocks_per_step), :] = means[kv]

    @pl.when(s == pl.num_programs(1) - 1)
    def _():
        q = q_ref[...]
        head = lax.broadcasted_iota(jnp.int32, (N_HEADS, n_blocks), 0)
        gate = jnp.zeros((N_HEADS, n_blocks), F32)
        for kv in range(N_KV_HEADS):
            g_kv = _dot_t(q, kmean_ref[kv], precision=lax.Precision.HIGHEST)
            gate = jnp.where((head >= kv * GROUP) & (head < (kv + 1) * GROUP), g_kv, gate)
        blk = lax.broadcasted_iota(jnp.int32, (N_HEADS, n_blocks), 1).astype(F32)
        lane = lax.broadcasted_iota(jnp.int32, sel_ref.shape, 1)
        picks = jnp.zeros(sel_ref.shape, F32)
        for t in range(MOBA_TOPK):
            best = jnp.max(gate, axis=-1, keepdims=True)
            idx = jnp.min(jnp.where(gate == best, blk, float(n_blocks)), axis=-1, keepdims=True)
            picks = jnp.where(lane == t, idx, picks)
            gate = jnp.where(blk == idx, -jnp.inf, gate)
        sel_ref[...] = picks.astype(jnp.int32)


def sample_moba_select(q, cache_k, page_table, layer_a):
    b = q.shape[0]
    n_pages = page_table.shape[1]
    n_blocks = n_pages // PAGES_PER_BLOCK
    steps = n_pages // GATE_PAGES_PER_STEP

    def page_spec(r):
        return pl.BlockSpec(
            (None, None, PAGE_ROWS, HEAD_DIM),
            lambda i, s, pt: (layer_a, pt[i, s * GATE_PAGES_PER_STEP + r], 0, 0))

    grid_spec = pltpu.PrefetchScalarGridSpec(
        num_scalar_prefetch=1,
        grid=(b, steps),
        in_specs=[pl.BlockSpec((None, N_HEADS, HEAD_DIM), lambda i, s, pt: (i, 0, 0))]
        + [page_spec(r) for r in range(GATE_PAGES_PER_STEP)],
        out_specs=pl.BlockSpec((None, N_HEADS, HEAD_DIM), lambda i, s, pt: (i, 0, 0)),
        scratch_shapes=[pltpu.VMEM((N_KV_HEADS, n_blocks, HEAD_DIM), F32)],
    )
    sel = pl.pallas_call(
        _moba_gate_kernel,
        grid_spec=grid_spec,
        out_shape=jax.ShapeDtypeStruct((b, N_HEADS, HEAD_DIM), jnp.int32),
        compiler_params=_params(2),
        name="sample_moba_select",
    )(page_table, q, *([cache_k] * GATE_PAGES_PER_STEP))
    return sel[:, :, :MOBA_TOPK]


N_SEL_PAGES = GROUP * MOBA_TOPK * PAGES_PER_BLOCK


def _moba_sample_kernel(pt_ref, sel_ref, slopes_ref, q_ref, kn_ref, vn_ref, *refs, past_len):
    del pt_ref
    k_pages = refs[:N_SEL_PAGES]
    v_pages = refs[N_SEL_PAGES:2 * N_SEL_PAGES]
    o_ref = refs[2 * N_SEL_PAGES]
    b = pl.program_id(0)
    kvh = pl.program_id(1)
    within = lax.broadcasted_iota(jnp.int32, (PAGE_SIZE, 1), 0)
    for g in range(GROUP):
        head = kvh * GROUP + g
        slope = slopes_ref[head]
        q_row = q_ref[g:g + 1, :]
        s_own = _row_scores(kn_ref[...], q_row)
        scores = []
        m = s_own
        for t in range(MOBA_TOPK):
            first = sel_ref[(b * N_HEADS + head) * MOBA_TOPK + t] * MOBA_BLOCK
            for r in range(PAGES_PER_BLOCK):
                page = k_pages[(g * MOBA_TOPK + t) * PAGES_PER_BLOCK + r]
                dist = (past_len - (first + r * PAGE_SIZE) - within).astype(F32)
                s = _row_scores(_head_rows(page, kvh, PAGE_SIZE, N_KV_HEADS), q_row) - slope * dist
                scores.append(s)
                m = jnp.maximum(m, jnp.max(s, axis=0, keepdims=True))
        p_own = jnp.exp(s_own - m)
        l = p_own
        o = p_own * vn_ref[...]
        for idx, s in enumerate(scores):
            p = jnp.exp(s - m)
            l = l + jnp.sum(p, axis=0, keepdims=True)
            v = _head_rows(v_pages[g * MOBA_TOPK * PAGES_PER_BLOCK + idx], kvh, PAGE_SIZE, N_KV_HEADS)
            o = o + jnp.sum(p * v, axis=0, keepdims=True)
        o_ref[g:g + 1, :] = o / l


def sample_moba_attention(q, k_new, v_new, cache_k, cache_v, page_table, sel, layer_a):
    b = q.shape[0]
    past_len = page_table.shape[1] * PAGE_SIZE

    def page_spec(g, t, r):
        def index_map(i, h, pt, sl):
            block = sl[(i * N_HEADS + h * GROUP + g) * MOBA_TOPK + t]
            return (layer_a, pt[i, block * PAGES_PER_BLOCK + r], 0, 0)
        return pl.BlockSpec((None, None, PAGE_ROWS, HEAD_DIM), index_map)

    page_specs = [page_spec(g, t, r) for g in range(GROUP) for t in range(MOBA_TOPK)
                  for r in range(PAGES_PER_BLOCK)]
    q_spec = pl.BlockSpec((None, None, GROUP, HEAD_DIM), lambda i, h, pt, sl: (i, h, 0, 0))
    row_spec = pl.BlockSpec((None, None, 1, HEAD_DIM), lambda i, h, pt, sl: (i, h, 0, 0))
    grid_spec = pltpu.PrefetchScalarGridSpec(
        num_scalar_prefetch=2,
        grid=(b, N_KV_HEADS),
        in_specs=[pl.BlockSpec(memory_space=pltpu.SMEM), q_spec, row_spec, row_spec]
        + page_specs + page_specs,
        out_specs=q_spec,
    )
    return pl.pallas_call(
        functools.partial(_moba_sample_kernel, past_len=past_len),
        grid_spec=grid_spec,
        out_shape=jax.ShapeDtypeStruct((b, N_KV_HEADS, GROUP, HEAD_DIM), F32),
        compiler_params=_params(2),
        name="sample_moba",
    )(page_table, sel.reshape(-1), jnp.asarray(SLOPES), q, k_new, v_new,
      *([cache_k] * N_SEL_PAGES), *([cache_v] * N_SEL_PAGES))


DENSE_TM = 1024
DOWN_TM = 512


def _half_ffn(x, xg, ssq, gain, wg_all, wu_all, wd_all, layer, next_gain):
    a, a_tail = ffn_up(xg, ssq, x, gain, wg_all, wu_all, layer, tm=DENSE_TM, tn=512)
    return matmul_residual([a], [a_tail], wd_all, layer, x, 0.5, tm=DOWN_TM, tn=512,
                           next_gain=next_gain)


def kernel(x_prompt, x_sample, cache_moba_k, cache_moba_v, cache_swa_k, cache_swa_v, cache_mem_k, cache_mem_v, page_table, mem_prompt, g_ffn1, w_ffn1_gate, w_ffn1_up, w_ffn1_down, g_attn, w_in, w_out, sinks, g_mem, w_mem_kv, g_ffn2, w_ffn2_gate, w_ffn2_up, w_ffn2_down, g_final):
    bp, seq, d = x_prompt.shape
    bs = x_sample.shape[0]
    m = bp * seq
    n_pool = cache_moba_k.shape[1]
    mem = mem_prompt.reshape(bp * N_MEM, d)
    mem_k_cache = cache_mem_k.reshape(DEPTH, bs, N_MEM * N_MEM_HEADS, HEAD_DIM)
    mem_v_cache = cache_mem_v.reshape(DEPTH, bs, N_MEM * N_MEM_HEADS, HEAD_DIM)
    swa_k_cache = cache_swa_k.reshape(-1, bs, WINDOW * N_KV_HEADS, HEAD_DIM)
    swa_v_cache = cache_swa_v.reshape(-1, bs, WINDOW * N_KV_HEADS, HEAD_DIM)
    moba_k_flat = cache_moba_k.reshape(-1, n_pool, PAGE_ROWS, HEAD_DIM)
    moba_v_flat = cache_moba_v.reshape(-1, n_pool, PAGE_ROWS, HEAD_DIM)

    moba_qkv, moba_ks, moba_vs = [], [], []
    swa_kp, swa_vp, swa_ks, swa_vs = [], [], [], []
    mem_kp, mem_vp = [], []
    x, xg, ssq = stream_start(x_prompt.reshape(m, d), x_sample.reshape(bs, d), g_ffn1, 0,
                              tm=DOWN_TM)
    for l in range(DEPTH):
        x, xg, ssq = _half_ffn(x, xg, ssq, (g_ffn1, l), w_ffn1_gate, w_ffn1_up, w_ffn1_down, l,
                               (g_attn, l))

        qkv = norm_project(xg, ssq, x, (g_attn, l), w_in, l, tm=DENSE_TM, tn=1024)
        qkv_s = qkv[m:]
        memkv = project(rms_norm(mem, g_mem, l, BF16), w_mem_kv, l, tn=512)
        mem_kp.append(memkv[:, :MQ_W].reshape(bp, N_MEM, N_MEM_HEADS, HEAD_DIM))
        mem_vp.append(memkv[:, MQ_W:].reshape(bp, N_MEM, N_MEM_HEADS, HEAD_DIM))

        qs = qkv_s[:, :Q_W]
        ks = qkv_s[:, Q_W:Q_W + KV_W]
        vs = qkv_s[:, Q_W + KV_W:Q_W + 2 * KV_W]
        qms = qkv_s[:, Q_W + 2 * KV_W:]
        ks4 = ks.reshape(bs, 1, N_KV_HEADS, HEAD_DIM)
        vs4 = vs.reshape(bs, 1, N_KV_HEADS, HEAD_DIM)

        oms = sample_memory_attention(qms.reshape(bs, 1, MQ_W), mem_k_cache, mem_v_cache, l)
        j = l // 2
        if l % 2 == 0:
            osp, omp = prompt_attention(qkv, memkv, bp, seq, None)
            sel = sample_moba_select(qs.reshape(bs, N_HEADS, HEAD_DIM), moba_k_flat, page_table, j)
            oss = sample_moba_attention(
                qs.reshape(bs, N_KV_HEADS, GROUP, HEAD_DIM),
                ks.reshape(bs, N_KV_HEADS, 1, HEAD_DIM), vs.reshape(bs, N_KV_HEADS, 1, HEAD_DIM),
                moba_k_flat, moba_v_flat, page_table, sel, j)
            moba_qkv.append(qkv)
            moba_ks.append(ks4)
            moba_vs.append(vs4)
        else:
            osp, omp = prompt_attention(qkv, memkv, bp, seq, sinks[j])
            oss = sample_swa_attention(qs.reshape(bs, 1, Q_W), ks.reshape(bs, 1, KV_W),
                                       vs.reshape(bs, 1, KV_W), swa_k_cache, swa_v_cache,
                                       sinks[j], j)
            last = jnp.stack([qkv[(b + 1) * seq - WINDOW:(b + 1) * seq, Q_W:Q_W + 2 * KV_W]
                              for b in range(bp)])
            swa_kp.append(last[:, :, :KV_W].reshape(bp, WINDOW, N_KV_HEADS, HEAD_DIM))
            swa_vp.append(last[:, :, KV_W:].reshape(bp, WINDOW, N_KV_HEADS, HEAD_DIM))
            swa_ks.append(jnp.concatenate([cache_swa_k[j], ks4], axis=1)[:, -WINDOW:])
            swa_vs.append(jnp.concatenate([cache_swa_v[j], vs4], axis=1)[:, -WINDOW:])

        x, xg, ssq = matmul_residual(
            [osp, omp], [oss.reshape(bs, Q_W), oms.reshape(bs, MQ_W)], w_out, l, x, 1.0,
            tm=DENSE_TM, tn=1024, next_gain=(g_ffn2, l))

        next_gain = (g_ffn1, l + 1) if l + 1 < DEPTH else None
        out = _half_ffn(x, xg, ssq, (g_ffn2, l), w_ffn2_gate, w_ffn2_up, w_ffn2_down, l,
                        next_gain)
        if next_gain is None:
            (x,) = out
        else:
            x, xg, ssq = out

    y_prompt, y_sample = stream_end(x, g_final, m, tm=DOWN_TM)
    moba_kp, moba_vp = export_prompt_kv(moba_qkv, m, tm=DOWN_TM)
    kv_shape = (len(moba_qkv), bp, seq, N_KV_HEADS, HEAD_DIM)
    return (y_prompt.reshape(bp, seq, d), y_sample.reshape(bs, 1, d),
            moba_kp.reshape(kv_shape), moba_vp.reshape(kv_shape),
            jnp.stack(moba_ks), jnp.stack(moba_vs),
            jnp.stack(swa_kp), jnp.stack(swa_vp), jnp.stack(swa_ks), jnp.stack(swa_vs),
            jnp.stack(mem_kp), jnp.stack(mem_vp))
```

```python
import functools
import math

import jax
import jax.numpy as jnp
import numpy as np
from jax import lax
from jax.experimental import pallas as pl
from jax.experimental.pallas import tpu as pltpu

F32 = jnp.float32
BF16 = jnp.bfloat16

D_MODEL = 2048
DEPTH = 4
PAGE_SIZE = 128
HEAD_DIM = 128
N_HEADS = 12
N_KV_HEADS = 4
GROUP = N_HEADS // N_KV_HEADS
N_MEM_HEADS = 4
N_MEM = 256
D_FF = 5632
MOBA_BLOCK = 256
MOBA_TOPK = 3
WINDOW = 128
RMS_EPS = 1e-6
Q_W = N_HEADS * HEAD_DIM
KV_W = N_KV_HEADS * HEAD_DIM
MQ_W = N_MEM_HEADS * HEAD_DIM
IN_W = Q_W + 2 * KV_W + MQ_W
SCALE = HEAD_DIM ** -0.5
PAGES_PER_BLOCK = MOBA_BLOCK // PAGE_SIZE

VMEM_LIMIT_BYTES = 56 * 1024 * 1024
MASKED = -1e30
Q_TILE = 256
GATE_PAGES_PER_STEP = 32


def _alibi_slope_list(n):
    def pow2(m):
        start = 2.0 ** (-(2.0 ** -(math.log2(m) - 3)))
        return [start ** (i + 1) for i in range(m)]
    if math.log2(n).is_integer():
        return pow2(n)
    c = 2 ** math.floor(math.log2(n))
    return pow2(c) + _alibi_slope_list(2 * c)[0::2][:n - c]


SLOPES = np.array(_alibi_slope_list(N_HEADS), dtype=np.float32)


def _params(n_grid_dims):
    return pltpu.CompilerParams(
        dimension_semantics=("arbitrary",) * n_grid_dims,
        vmem_limit_bytes=VMEM_LIMIT_BYTES)


def _dot_t(a, b, precision=None):
    return lax.dot_general(a, b, (((1,), (1,)), ((), ())), precision=precision,
                           preferred_element_type=F32)


def _norm_kernel(x_ref, g_ref, o_ref):
    x = x_ref[...]
    ms = jnp.mean(x * x, axis=-1, keepdims=True)
    o_ref[...] = (x * lax.rsqrt(ms + RMS_EPS) * g_ref[...]).astype(o_ref.dtype)


def rms_norm(x, g_all, layer, out_dtype):
    m, d = x.shape
    tm = min(m, 512)
    g3 = g_all.reshape(g_all.shape[0], 1, d)
    return pl.pallas_call(
        _norm_kernel,
        grid=(m // tm,),
        in_specs=[pl.BlockSpec((tm, d), lambda i: (i, 0)),
                  pl.BlockSpec((None, 1, d), lambda i: (layer, 0, 0))],
        out_specs=pl.BlockSpec((tm, d), lambda i: (i, 0)),
        out_shape=jax.ShapeDtypeStruct((m, d), out_dtype),
        compiler_params=_params(1),
        name="rms_norm",
    )(x, g3)


def _proj_kernel(h_ref, w_ref, o_ref, wb_ref):
    @pl.when(pl.program_id(1) == 0)
    def _():
        wb_ref[...] = w_ref[...].astype(BF16)
    o_ref[...] = jnp.dot(h_ref[...], wb_ref[...], preferred_element_type=F32)


def project(h, w_all, layer, tn):
    m, k = h.shape
    n = w_all.shape[2]
    tm = min(m, 1024)
    return pl.pallas_call(
        _proj_kernel,
        grid=(n // tn, m // tm),
        in_specs=[pl.BlockSpec((tm, k), lambda j, i: (i, 0)),
                  pl.BlockSpec((None, k, tn), lambda j, i: (layer, 0, j))],
        out_specs=pl.BlockSpec((tm, tn), lambda j, i: (i, j)),
        out_shape=jax.ShapeDtypeStruct((m, n), F32),
        scratch_shapes=[pltpu.VMEM((k, tn), BF16)],
        compiler_params=_params(2),
        name="project",
    )(h, w_all)


LANES = 128


def _tail_first(i, n_full):
    return jnp.where(i == 0, n_full, i - 1)


def _main_tile(i):
    return jnp.maximum(i - 1, 0)


def _lane_partial_sums(x):
    total = x[:, :LANES]
    for c in range(1, x.shape[1] // LANES):
        total = total + x[:, c * LANES:(c + 1) * LANES]
    return total


def _store_norm_inputs(x, g_ref, xg_ref, ssq_ref):
    xg_ref[...] = (x * g_ref[...]).astype(xg_ref.dtype)
    ssq_ref[...] = _lane_partial_sums(x * x)


def _inv_rms(ssq, d_model):
    total = jnp.sum(jnp.sum(ssq, axis=0), axis=-1, keepdims=True)
    return lax.rsqrt(total * (1.0 / d_model) + RMS_EPS)


def _normed_rows(x, g):
    return x * lax.rsqrt(jnp.mean(x * x, axis=-1, keepdims=True) + RMS_EPS) * g


def _dot_split(a, w, w_hi):
    a_hi = a.astype(BF16)
    a_lo = (a - a_hi.astype(F32)).astype(BF16)
    w_lo = (w - w_hi.astype(F32)).astype(BF16)
    t = a.shape[0]
    on_hi = jnp.dot(jnp.concatenate([a_hi, a_lo], axis=0), w_hi, preferred_element_type=F32)
    return on_hi[:t] + on_hi[t:] + jnp.dot(a_hi, w_lo, preferred_element_type=F32)


def _stream_start_kernel(xp_ref, xs_ref, g_ref, x_ref, xg_ref, ssq_ref, *, tail):
    i = pl.program_id(0)

    @pl.when(i == 0)
    def _():
        x_ref[0:tail, :] = xs_ref[...]

    @pl.when(i > 0)
    def _():
        x = xp_ref[...]
        x_ref[...] = x
        _store_norm_inputs(x, g_ref, xg_ref, ssq_ref)


def stream_start(x_main, x_tail, g_all, layer, tm):
    m, d = x_main.shape
    tail = x_tail.shape[0]
    n_full = m // tm
    g3 = g_all.reshape(g_all.shape[0], 1, d)
    main_block = lambda i: (_main_tile(i), 0)
    return pl.pallas_call(
        functools.partial(_stream_start_kernel, tail=tail),
        grid=(n_full + 1,),
        in_specs=[pl.BlockSpec((tm, d), main_block),
                  pl.BlockSpec((tail, d), lambda i: (0, 0)),
                  pl.BlockSpec((None, 1, d), lambda i: (layer, 0, 0))],
        out_specs=[pl.BlockSpec((tm, d), lambda i: (_tail_first(i, n_full), 0)),
                   pl.BlockSpec((tm, d), main_block),
                   pl.BlockSpec((None, tm, LANES), lambda i: (0, _main_tile(i), 0))],
        out_shape=[jax.ShapeDtypeStruct((m + tail, d), F32), jax.ShapeDtypeStruct((m, d), BF16),
                   jax.ShapeDtypeStruct((1, m, LANES), F32)],
        compiler_params=_params(1),
        name="stream_start",
    )(x_main, x_tail, g3)


def _stream_end_kernel(x_ref, g_ref, yp_ref, ys_ref, *, tail):
    i = pl.program_id(0)

    @pl.when(i == 0)
    def _():
        ys_ref[...] = _normed_rows(x_ref[0:tail, :], g_ref[...])

    @pl.when(i > 0)
    def _():
        yp_ref[...] = _normed_rows(x_ref[...], g_ref[...])


def stream_end(x, g, m, tm):
    rows, d = x.shape
    tail = rows - m
    n_full = m // tm
    return pl.pallas_call(
        functools.partial(_stream_end_kernel, tail=tail),
        grid=(n_full + 1,),
        in_specs=[pl.BlockSpec((tm, d), lambda i: (_tail_first(i, n_full), 0)),
                  pl.BlockSpec((1, d), lambda i: (0, 0))],
        out_specs=[pl.BlockSpec((tm, d), lambda i: (_main_tile(i), 0)),
                   pl.BlockSpec((tail, d), lambda i: (0, 0))],
        out_shape=[jax.ShapeDtypeStruct((m, d), F32), jax.ShapeDtypeStruct((tail, d), F32)],
        compiler_params=_params(1),
        name="stream_end",
    )(x, g.reshape(1, d))


def _normed_specs(x, gain, nj, tm, n_full):
    rows, k = x.shape
    tail = rows - n_full * tm
    g_all, g_layer = gain
    specs = [pl.BlockSpec((tm, k), lambda j, i: (_main_tile(i), 0)),
             pl.BlockSpec((nj, tm, LANES),
                          lambda j, i: (0, jnp.where(j == 0, _main_tile(i), n_full - 1), 0)),
             pl.BlockSpec((tail, k), lambda j, i: (n_full * tm // tail, 0)),
             pl.BlockSpec((None, 1, k), lambda j, i: (g_layer, 0, 0))]
    return specs, [x, g_all.reshape(-1, 1, k)]


def _cached_inv_rms(ssq_ref, inv_ref, j, i, d_model):
    @pl.when(j == 0)
    def _():
        inv_ref[i] = _inv_rms(ssq_ref[...], d_model)
    return inv_ref[i]


def _norm_proj_kernel(xg_ref, ssq_ref, xt_ref, g_ref, w_ref, o_ref, wb_ref, inv_ref, *, tail):
    j = pl.program_id(0)
    i = pl.program_id(1)

    @pl.when(i == 0)
    def _():
        wb_ref[...] = w_ref[...].astype(BF16)
        h = _normed_rows(xt_ref[...], g_ref[...])
        o_ref[0:tail, :] = _dot_split(h, w_ref[...], wb_ref[...])

    @pl.when(i > 0)
    def _():
        inv = _cached_inv_rms(ssq_ref, inv_ref, j, i, xg_ref.shape[1])
        o_ref[...] = inv * jnp.dot(xg_ref[...], wb_ref[...], preferred_element_type=F32)


def norm_project(xg, ssq, x, gain, w_all, layer, tm, tn):
    m, k = xg.shape
    rows = x.shape[0]
    n = w_all.shape[2]
    n_full = m // tm
    specs, operands = _normed_specs(x, gain, ssq.shape[0], tm, n_full)
    return pl.pallas_call(
        functools.partial(_norm_proj_kernel, tail=rows - m),
        grid=(n // tn, n_full + 1),
        in_specs=specs + [pl.BlockSpec((None, k, tn), lambda j, i: (layer, 0, j))],
        out_specs=pl.BlockSpec((tm, tn), lambda j, i: (_tail_first(i, n_full), j)),
        out_shape=jax.ShapeDtypeStruct((rows, n), F32),
        scratch_shapes=[pltpu.VMEM((k, tn), BF16), pltpu.VMEM((n_full + 1, tm, 1), F32)],
        compiler_params=_params(2),
        name="norm_project",
    )(xg, ssq, *operands, w_all)


def _ffn_up_kernel(xg_ref, ssq_ref, xt_ref, g_ref, wg_ref, wu_ref, a_ref, at_ref,
                   wgb_ref, wub_ref, inv_ref):
    j = pl.program_id(0)
    i = pl.program_id(1)

    def swiglu(g, u):
        return g * jax.nn.sigmoid(g) * u

    @pl.when(i == 0)
    def _():
        wgb_ref[...] = wg_ref[...].astype(BF16)
        wub_ref[...] = wu_ref[...].astype(BF16)
        h = _normed_rows(xt_ref[...], g_ref[...])
        at_ref[...] = swiglu(_dot_split(h, wg_ref[...], wgb_ref[...]),
                             _dot_split(h, wu_ref[...], wub_ref[...]))

    @pl.when(i > 0)
    def _():
        inv = _cached_inv_rms(ssq_ref, inv_ref, j, i, xg_ref.shape[1])
        h = xg_ref[...]
        g = inv * jnp.dot(h, wgb_ref[...], preferred_element_type=F32)
        u = inv * jnp.dot(h, wub_ref[...], preferred_element_type=F32)
        a_ref[...] = swiglu(g, u).astype(a_ref.dtype)


def ffn_up(xg, ssq, x, gain, wg_all, wu_all, layer, tm, tn):
    m, k = xg.shape
    tail = x.shape[0] - m
    n = wg_all.shape[2]
    n_full = m // tm
    specs, operands = _normed_specs(x, gain, ssq.shape[0], tm, n_full)
    w_spec = pl.BlockSpec((None, k, tn), lambda j, i: (layer, 0, j))
    return pl.pallas_call(
        _ffn_up_kernel,
        grid=(n // tn, n_full + 1),
        in_specs=specs + [w_spec, w_spec],
        out_specs=[pl.BlockSpec((tm, tn), lambda j, i: (_main_tile(i), j)),
                   pl.BlockSpec((tail, tn), lambda j, i: (0, j))],
        out_shape=[jax.ShapeDtypeStruct((m, n), BF16), jax.ShapeDtypeStruct((tail, n), F32)],
        scratch_shapes=[pltpu.VMEM((k, tn), BF16), pltpu.VMEM((k, tn), BF16),
                        pltpu.VMEM((n_full + 1, tm, 1), F32)],
        compiler_params=_params(2),
        name="ffn_up",
    )(xg, ssq, *operands, wg_all, wu_all)


def _mm_res_kernel(*refs, k_sizes, emit_norm_inputs, scale, tail):
    n_a = len(k_sizes)
    main_refs = refs[:n_a]
    tail_refs = refs[n_a:2 * n_a]
    if emit_norm_inputs:
        w_ref, x_ref, g_ref, o_ref, xg_ref, ssq_ref, wb_ref = refs[2 * n_a:]
    else:
        w_ref, x_ref, o_ref, wb_ref = refs[2 * n_a:]
    i = pl.program_id(1)

    def product(a_refs, dot):
        acc = None
        off = 0
        for a_ref, ksz in zip(a_refs, k_sizes):
            d = dot(a_ref[...], slice(off, off + ksz))
            acc = d if acc is None else acc + d
            off += ksz
        return acc

    @pl.when(i == 0)
    def _():
        wb_ref[...] = w_ref[...].astype(BF16)
        acc = product(tail_refs, lambda a, rows: _dot_split(a, w_ref[rows, :], wb_ref[rows, :]))
        o_ref[0:tail, :] = x_ref[0:tail, :] + scale * acc

    @pl.when(i > 0)
    def _():
        acc = product(main_refs, lambda a, rows: jnp.dot(a.astype(BF16), wb_ref[rows, :],
                                                         preferred_element_type=F32))
        x = x_ref[...] + scale * acc
        o_ref[...] = x
        if emit_norm_inputs:
            _store_norm_inputs(x, g_ref, xg_ref, ssq_ref)


def matmul_residual(a_main, a_tail, w_all, layer, x, scale, tm, tn, next_gain=None):
    rows, n = x.shape
    m = a_main[0].shape[0]
    tail = rows - m
    n_full = m // tm
    k_sizes = tuple(a.shape[1] for a in a_main)
    k = sum(k_sizes)
    nj = n // tn
    a_specs = [pl.BlockSpec((tm, ks), lambda j, i: (_main_tile(i), 0)) for ks in k_sizes]
    a_specs += [pl.BlockSpec((tail, ks), lambda j, i: (0, 0)) for ks in k_sizes]
    tile = pl.BlockSpec((tm, tn), lambda j, i: (_tail_first(i, n_full), j))
    in_specs = a_specs + [pl.BlockSpec((None, k, tn), lambda j, i: (layer, 0, j)), tile]
    operands = list(a_main) + list(a_tail) + [w_all, x]
    out_specs = [tile]
    out_shape = [jax.ShapeDtypeStruct((rows, n), F32)]
    emit = next_gain is not None
    if emit:
        g_all, g_layer = next_gain
        in_specs.append(pl.BlockSpec((None, 1, tn), lambda j, i: (g_layer, 0, j)))
        operands.append(g_all.reshape(-1, 1, n))
        out_specs += [pl.BlockSpec((tm, tn), lambda j, i: (_main_tile(i), j)),
                      pl.BlockSpec((None, tm, LANES), lambda j, i: (j, _main_tile(i), 0))]
        out_shape += [jax.ShapeDtypeStruct((m, n), BF16),
                      jax.ShapeDtypeStruct((nj, m, LANES), F32)]
    return pl.pallas_call(
        functools.partial(_mm_res_kernel, k_sizes=k_sizes, emit_norm_inputs=emit, scale=scale,
                          tail=tail),
        grid=(nj, n_full + 1),
        in_specs=in_specs, out_specs=out_specs, out_shape=out_shape,
        scratch_shapes=[pltpu.VMEM((k, tn), BF16)],
        compiler_params=_params(2),
        name="matmul_residual",
    )(*operands)


def _softmax_pv(s, v):
    m = jnp.max(s, axis=-1, keepdims=True)
    p = jnp.exp(s - m)
    l = jnp.sum(p, axis=-1, keepdims=True)
    return jnp.dot(p.astype(BF16), v, preferred_element_type=F32), m, l


def _memory_attention_tile(qm_ref, mk_ref, mv_ref, om_ref):
    qm = (qm_ref[...] * SCALE).astype(BF16)
    s = _dot_t(qm, mk_ref[...].astype(BF16))
    o, _, l = _softmax_pv(s, mv_ref[...].astype(BF16))
    om_ref[...] = (o / l).astype(om_ref.dtype)


def _moba_prompt_kernel(slopes_ref, q_ref, k_ref, v_ref, qm_ref, mk_ref, mv_ref,
                        os_ref, om_ref, kb_ref, vt_ref, kmean_ref, bias_ref, nslope_ref,
                        chosen_ref, sa_ref, sb_ref, acc_ref, m_ref, l_ref):
    kvh = pl.program_id(1)
    i = pl.program_id(2)
    nb = kmean_ref.shape[0]
    tq = q_ref.shape[0]
    width = GROUP * tq
    key = lax.broadcasted_iota(jnp.int32, (MOBA_BLOCK, width), 0)
    query = lax.broadcasted_iota(jnp.int32, (MOBA_BLOCK, width), 1) & (tq - 1)

    @pl.when(i == 0)
    def _():
        kb_ref[...] = k_ref[...].astype(BF16)
        for n in range(nb):
            rows = slice(n * MOBA_BLOCK, (n + 1) * MOBA_BLOCK)
            vt_ref[n] = v_ref[rows, :].T.astype(BF16)
            kmean_ref[n:n + 1, :] = jnp.mean(k_ref[rows, :], axis=0, keepdims=True)
        lane = lax.broadcasted_iota(jnp.int32, (1, width), 1)
        nslope = jnp.zeros((1, width), F32)
        for g in range(GROUP):
            nslope = jnp.where((lane >= g * tq) & (lane < (g + 1) * tq),
                               -slopes_ref[kvh * GROUP + g], nslope)
        nslope_ref[...] = nslope
        bias_ref[...] = nslope * (query - key).astype(F32)

    q_t = jnp.concatenate(
        [q_ref[:, g * HEAD_DIM:(g + 1) * HEAD_DIM] for g in range(GROUP)], axis=0).T
    gate = jnp.dot(kmean_ref[...], q_t, precision=lax.Precision.HIGHEST,
                   preferred_element_type=F32)
    blk = lax.broadcasted_iota(jnp.int32, (nb, width), 0)
    rank = jnp.zeros((nb, width), jnp.int32)
    for m in range(nb - 1):
        gm = gate[m:m + 1, :]
        beats = (gm > gate) | ((gm == gate) & (m < blk))
        rank = rank + jnp.where(beats & (m < i), 1, 0)
    chosen_ref[...] = jnp.where((blk < i) & (rank < MOBA_TOPK), 1.0, 0.0)

    qb_t = (q_t * SCALE).astype(BF16)

    def scores(block):
        start = pl.multiple_of(block * MOBA_BLOCK, MOBA_BLOCK)
        return jnp.dot(kb_ref[pl.ds(start, MOBA_BLOCK), :], qb_t, preferred_element_type=F32)

    sa_ref[...] = scores(0)
    s = jnp.where(query >= key, scores(i) + bias_ref[...], MASKED)
    m_own = jnp.max(s, axis=0, keepdims=True)
    p = jnp.exp(s - m_own)
    m_ref[...] = m_own
    l_ref[...] = jnp.sum(p, axis=0, keepdims=True)
    acc_ref[...] = jnp.dot(vt_ref[i], p.astype(BF16), preferred_element_type=F32)

    def absorb(s_ref, block):
        offset = ((i - block) * MOBA_BLOCK).astype(F32)
        shift = jnp.where(chosen_ref[pl.ds(block, 1), :] > 0.0, nslope_ref[...] * offset, MASKED)
        s = s_ref[...] + bias_ref[...] + shift
        m_old = m_ref[...]
        m_new = jnp.maximum(m_old, jnp.max(s, axis=0, keepdims=True))
        alpha = jnp.exp(m_old - m_new)
        p = jnp.exp(s - m_new)
        m_ref[...] = m_new
        l_ref[...] = alpha * l_ref[...] + jnp.sum(p, axis=0, keepdims=True)
        acc_ref[...] = alpha * acc_ref[...] + jnp.dot(vt_ref[block], p.astype(BF16),
                                                      preferred_element_type=F32)

    def past_pair(t, carry):
        first = 2 * t
        second = jnp.minimum(first + 1, nb - 1)
        sb_ref[...] = scores(second)
        absorb(sa_ref, first)
        sa_ref[...] = scores(jnp.minimum(first + 2, nb - 1))
        absorb(sb_ref, second)
        return carry

    lax.fori_loop(0, (i + 1) // 2, past_pair, 0)
    out_t = acc_ref[...] / l_ref[...]
    for g in range(GROUP):
        os_ref[:, g * HEAD_DIM:(g + 1) * HEAD_DIM] = (
            out_t[:, g * tq:(g + 1) * tq].T.astype(os_ref.dtype))

    _memory_attention_tile(qm_ref, mk_ref, mv_ref, om_ref)


def _swa_prompt_kernel(slopes_ref, sinks_ref, q_ref, k_ref, v_ref, qm_ref, mk_ref, mv_ref,
                       os_ref, om_ref, kb_ref, vt_ref, bias_ref, sink_ref):
    kvh = pl.program_id(1)
    i = pl.program_id(2)
    tq = q_ref.shape[0]
    width = GROUP * tq
    n_chunks = tq // WINDOW + 1
    span = n_chunks * WINDOW

    @pl.when(i == 0)
    def _():
        kb_ref[...] = k_ref[...].astype(BF16)
        for n in range(vt_ref.shape[0]):
            vt_ref[n] = v_ref[n * WINDOW:(n + 1) * WINDOW, :].T.astype(BF16)
        lane = lax.broadcasted_iota(jnp.int32, (1, width), 1)
        nslope = jnp.zeros((1, width), F32)
        sink = jnp.zeros((1, width), F32)
        for g in range(GROUP):
            in_head = (lane >= g * tq) & (lane < (g + 1) * tq)
            nslope = jnp.where(in_head, -slopes_ref[kvh * GROUP + g], nslope)
            sink = jnp.where(in_head, sinks_ref[kvh * GROUP + g], sink)
        sink_ref[...] = sink
        key = lax.broadcasted_iota(jnp.int32, (span, width), 0)
        query = lax.broadcasted_iota(jnp.int32, (span, width), 1) & (tq - 1)
        for placement, lead in enumerate((0, WINDOW)):
            dist = query - key + lead
            bias_ref[placement] = jnp.where((dist >= 0) & (dist <= WINDOW),
                                            nslope * dist.astype(F32), MASKED)

    first_chunk = jnp.maximum(i * (tq // WINDOW) - 1, 0)
    kstart = pl.multiple_of(first_chunk * WINDOW, WINDOW)
    q_t = jnp.concatenate(
        [q_ref[:, g * HEAD_DIM:(g + 1) * HEAD_DIM] for g in range(GROUP)], axis=0).T
    qb_t = (q_t * SCALE).astype(BF16)
    s = (jnp.dot(kb_ref[pl.ds(kstart, span), :], qb_t, preferred_element_type=F32)
         + bias_ref[jnp.minimum(i, 1)])
    sink = sink_ref[...]
    m = jnp.maximum(jnp.max(s, axis=0, keepdims=True), sink)
    p = jnp.exp(s - m)
    l = jnp.sum(p, axis=0, keepdims=True) + jnp.exp(sink - m)
    pb = p.astype(BF16)
    out_t = None
    for c in range(n_chunks):
        part = jnp.dot(vt_ref[first_chunk + c], pb[c * WINDOW:(c + 1) * WINDOW, :],
                       preferred_element_type=F32)
        out_t = part if out_t is None else out_t + part
    out_t = out_t / l
    for g in range(GROUP):
        os_ref[:, g * HEAD_DIM:(g + 1) * HEAD_DIM] = (
            out_t[:, g * tq:(g + 1) * tq].T.astype(os_ref.dtype))

    _memory_attention_tile(qm_ref, mk_ref, mv_ref, om_ref)


def prompt_attention(qkv, memkv, batch, seq, sinks_layer):
    nq = seq // Q_TILE
    k_col = Q_W // HEAD_DIM
    v_col = (Q_W + KV_W) // HEAD_DIM
    qm_col = (Q_W + 2 * KV_W) // HEAD_DIM
    smem = pl.BlockSpec(memory_space=pltpu.SMEM)
    tensor_specs = [
        pl.BlockSpec((Q_TILE, GROUP * HEAD_DIM), lambda b, h, i: (b * nq + i, h)),
        pl.BlockSpec((seq, HEAD_DIM), lambda b, h, i: (b, k_col + h)),
        pl.BlockSpec((seq, HEAD_DIM), lambda b, h, i: (b, v_col + h)),
        pl.BlockSpec((Q_TILE, HEAD_DIM), lambda b, h, i: (b * nq + i, qm_col + h)),
        pl.BlockSpec((N_MEM, HEAD_DIM), lambda b, h, i: (b, h)),
        pl.BlockSpec((N_MEM, HEAD_DIM), lambda b, h, i: (b, N_MEM_HEADS + h)),
    ]
    out_specs = [
        pl.BlockSpec((Q_TILE, GROUP * HEAD_DIM), lambda b, h, i: (b * nq + i, h)),
        pl.BlockSpec((Q_TILE, HEAD_DIM), lambda b, h, i: (b * nq + i, h)),
    ]
    out_shape = [jax.ShapeDtypeStruct((batch * seq, Q_W), BF16),
                 jax.ShapeDtypeStruct((batch * seq, MQ_W), BF16)]
    width = GROUP * Q_TILE
    row_scratch = pltpu.VMEM((1, width), F32)
    slopes = jnp.asarray(SLOPES)
    tensors = (qkv, qkv, qkv, qkv, memkv, memkv)
    if sinks_layer is None:
        return pl.pallas_call(
            _moba_prompt_kernel,
            grid=(batch, N_KV_HEADS, nq),
            in_specs=[smem] + tensor_specs,
            out_specs=out_specs, out_shape=out_shape,
            scratch_shapes=[
                pltpu.VMEM((seq, HEAD_DIM), BF16),
                pltpu.VMEM((seq // MOBA_BLOCK, HEAD_DIM, MOBA_BLOCK), BF16),
                pltpu.VMEM((seq // MOBA_BLOCK, HEAD_DIM), F32),
                pltpu.VMEM((MOBA_BLOCK, width), F32),
                row_scratch,
                pltpu.VMEM((seq // MOBA_BLOCK, width), F32),
                pltpu.VMEM((MOBA_BLOCK, width), F32),
                pltpu.VMEM((MOBA_BLOCK, width), F32),
                pltpu.VMEM((HEAD_DIM, width), F32),
                row_scratch,
                row_scratch,
            ],
            compiler_params=_params(3),
            name="moba_prompt",
        )(slopes, *tensors)
    return pl.pallas_call(
        _swa_prompt_kernel,
        grid=(batch, N_KV_HEADS, nq),
        in_specs=[smem, smem] + tensor_specs,
        out_specs=out_specs, out_shape=out_shape,
        scratch_shapes=[
            pltpu.VMEM((seq, HEAD_DIM), BF16),
            pltpu.VMEM((seq // WINDOW, HEAD_DIM, WINDOW), BF16),
            pltpu.VMEM((2, Q_TILE + WINDOW, width), F32),
            row_scratch,
        ],
        compiler_params=_params(3),
        name="swa_prompt",
    )(slopes, sinks_layer, *tensors)


def _export_kv_kernel(k0_ref, v0_ref, k1_ref, v1_ref, ok_ref, ov_ref):
    def export(k_ref, v_ref):
        for h in range(N_KV_HEADS):
            lanes = slice(h * HEAD_DIM, (h + 1) * HEAD_DIM)
            rows = pl.ds(h, k_ref.shape[0], stride=N_KV_HEADS)
            ok_ref[rows, :] = k_ref[:, lanes]
            ov_ref[rows, :] = v_ref[:, lanes]

    @pl.when(pl.program_id(0) == 0)
    def _():
        export(k0_ref, v0_ref)

    @pl.when(pl.program_id(0) == 1)
    def _():
        export(k1_ref, v1_ref)


def export_prompt_kv(qkv_pair, m, tm):
    k_col = Q_W // KV_W
    v_col = k_col + 1

    def spec(layer, col):
        return pl.BlockSpec((tm, KV_W), lambda a, i: (jnp.where(a == layer, i, 0), col))

    out_spec = pl.BlockSpec((None, tm * N_KV_HEADS, HEAD_DIM), lambda a, i: (a, i, 0))
    out = jax.ShapeDtypeStruct((2, m * N_KV_HEADS, HEAD_DIM), F32)
    q0, q1 = qkv_pair
    return pl.pallas_call(
        _export_kv_kernel,
        grid=(2, m // tm),
        in_specs=[spec(0, k_col), spec(0, v_col), spec(1, k_col), spec(1, v_col)],
        out_specs=[out_spec, out_spec], out_shape=[out, out],
        compiler_params=_params(2),
        name="export_prompt_kv",
    )(q0, q0, q1, q1)


def _head_rows(ref, head, n_tokens, n_heads):
    return ref[pl.ds(head, n_tokens, stride=n_heads), :]


def _row_scores(k, q_row):
    return jnp.sum(k * q_row, axis=-1, keepdims=True) * SCALE


def _sample_memory_kernel(qm_ref, mk_ref, mv_ref, o_ref):
    for h in range(N_MEM_HEADS):
        lanes = slice(h * HEAD_DIM, (h + 1) * HEAD_DIM)
        s = _row_scores(_head_rows(mk_ref, h, N_MEM, N_MEM_HEADS), qm_ref[:, lanes])
        p = jnp.exp(s - jnp.max(s, axis=0, keepdims=True))
        l = jnp.sum(p, axis=0, keepdims=True)
        v = _head_rows(mv_ref, h, N_MEM, N_MEM_HEADS)
        o_ref[:, lanes] = jnp.sum(p * v, axis=0, keepdims=True) / l


def sample_memory_attention(qm, mem_k, mem_v, layer):
    b = qm.shape[0]
    row = pl.BlockSpec((None, 1, MQ_W), lambda i: (i, 0, 0))
    cache = pl.BlockSpec((None, None, N_MEM * N_MEM_HEADS, HEAD_DIM), lambda i: (layer, i, 0, 0))
    return pl.pallas_call(
        _sample_memory_kernel,
        grid=(b,),
        in_specs=[row, cache, cache],
        out_specs=row,
        out_shape=jax.ShapeDtypeStruct((b, 1, MQ_W), F32),
        compiler_params=_params(1),
        name="sample_memory",
    )(qm, mem_k, mem_v)


def _sample_swa_kernel(sinks_ref, q_ref, kc_ref, vc_ref, kn_ref, vn_ref, o_ref):
    back = (WINDOW - lax.broadcasted_iota(jnp.int32, (WINDOW, 1), 0)).astype(F32)
    for hq in range(N_HEADS):
        kv = hq // GROUP
        q_row = q_ref[:, hq * HEAD_DIM:(hq + 1) * HEAD_DIM]
        lanes = slice(kv * HEAD_DIM, (kv + 1) * HEAD_DIM)
        sink = sinks_ref[hq]
        s_c = (_row_scores(_head_rows(kc_ref, kv, WINDOW, N_KV_HEADS), q_row)
               - float(SLOPES[hq]) * back)
        s_n = _row_scores(kn_ref[:, lanes], q_row)
        m = jnp.maximum(jnp.maximum(jnp.max(s_c, axis=0, keepdims=True), s_n), sink)
        p_c = jnp.exp(s_c - m)
        p_n = jnp.exp(s_n - m)
        l = jnp.sum(p_c, axis=0, keepdims=True) + p_n + jnp.exp(sink - m)
        v_c = _head_rows(vc_ref, kv, WINDOW, N_KV_HEADS)
        o = jnp.sum(p_c * v_c, axis=0, keepdims=True) + p_n * vn_ref[:, lanes]
        o_ref[:, hq * HEAD_DIM:(hq + 1) * HEAD_DIM] = o / l


def sample_swa_attention(q, k_new, v_new, cache_k, cache_v, sinks_layer, layer_b):
    b = q.shape[0]
    q_spec = pl.BlockSpec((None, 1, Q_W), lambda i: (i, 0, 0))
    kv_row = pl.BlockSpec((None, 1, KV_W), lambda i: (i, 0, 0))
    cache = pl.BlockSpec((None, None, WINDOW * N_KV_HEADS, HEAD_DIM),
                         lambda i: (layer_b, i, 0, 0))
    return pl.pallas_call(
        _sample_swa_kernel,
        grid=(b,),
        in_specs=[pl.BlockSpec(memory_space=pltpu.SMEM), q_spec, cache, cache, kv_row, kv_row],
        out_specs=q_spec,
        out_shape=jax.ShapeDtypeStruct((b, 1, Q_W), F32),
        compiler_params=_params(1),
        name="sample_swa",
    )(sinks_layer, q, cache_k, cache_v, k_new, v_new)


PAGE_ROWS = PAGE_SIZE * N_KV_HEADS
SUBLANES = 8


def _moba_gate_kernel(pt_ref, q_ref, *refs):
    del pt_ref
    pages = refs[:GATE_PAGES_PER_STEP]
    sel_ref, kmean_ref = refs[GATE_PAGES_PER_STEP:]
    s = pl.program_id(1)
    blocks_per_step = GATE_PAGES_PER_STEP // PAGES_PER_BLOCK
    n_blocks = kmean_ref.shape[1]

    sub = lax.broadcasted_iota(jnp.int32, (blocks_per_step, HEAD_DIM), 0)
    means = [jnp.zeros((blocks_per_step, HEAD_DIM), F32) for _ in range(N_KV_HEADS)]
    for r in range(blocks_per_step):
        total = None
        for page in pages[r * PAGES_PER_BLOCK:(r + 1) * PAGES_PER_BLOCK]:
            part = jnp.sum(page[...].reshape(PAGE_ROWS // SUBLANES, SUBLANES, HEAD_DIM), axis=0)
            total = part if total is None else total + part
        for kv in range(N_KV_HEADS):
            head_sum = total[kv:kv + 1, :] + total[kv + N_KV_HEADS:kv + N_KV_HEADS + 1, :]
            means[kv] = jnp.where(sub == r, head_sum * (1.0 / MOBA_BLOCK), means[kv])
    first = pl.multiple_of(s * blocks_per_step, blocks_per_step)
    for kv in range(N_KV_HEADS):
        kmean_ref[kv, pl.ds(first, blocks_per_step), :] = means[kv]

    @pl.when(s == pl.num_programs(1) - 1)
    def _():
        q = q_ref[...]
        head = lax.broadcasted_iota(jnp.int32, (N_HEADS, n_blocks), 0)
        gate = jnp.zeros((N_HEADS, n_blocks), F32)
        for kv in range(N_KV_HEADS):
            g_kv = _dot_t(q, kmean_ref[kv], precision=lax.Precision.HIGHEST)
            gate = jnp.where((head >= kv * GROUP) & (head < (kv + 1) * GROUP), g_kv, gate)
        blk = lax.broadcasted_iota(jnp.int32, (N_HEADS, n_blocks), 1).astype(F32)
        lane = lax.broadcasted_iota(jnp.int32, sel_ref.shape, 1)
        picks = jnp.zeros(sel_ref.shape, F32)
        for t in range(MOBA_TOPK):
            best = jnp.max(gate, axis=-1, keepdims=True)
            idx = jnp.min(jnp.where(gate == best, blk, float(n_blocks)), axis=-1, keepdims=True)
            picks = jnp.where(lane == t, idx, picks)
            gate = jnp.where(blk == idx, -jnp.inf, gate)
        sel_ref[...] = picks.astype(jnp.int32)


def sample_moba_select(q, cache_k, page_table, layer_a):
    b = q.shape[0]
    n_pages = page_table.shape[1]
    n_blocks = n_pages // PAGES_PER_BLOCK
    steps = n_pages // GATE_PAGES_PER_STEP

    def page_spec(r):
        return pl.BlockSpec(
            (None, None, PAGE_ROWS, HEAD_DIM),
            lambda i, s, pt: (layer_a, pt[i, s * GATE_PAGES_PER_STEP + r], 0, 0))

    grid_spec = pltpu.PrefetchScalarGridSpec(
        num_scalar_prefetch=1,
        grid=(b, steps),
        in_specs=[pl.BlockSpec((None, N_HEADS, HEAD_DIM), lambda i, s, pt: (i, 0, 0))]
        + [page_spec(r) for r in range(GATE_PAGES_PER_STEP)],
        out_specs=pl.BlockSpec((None, N_HEADS, HEAD_DIM), lambda i, s, pt: (i, 0, 0)),
        scratch_shapes=[pltpu.VMEM((N_KV_HEADS, n_blocks, HEAD_DIM), F32)],
    )
    sel = pl.pallas_call(
        _moba_gate_kernel,
        grid_spec=grid_spec,
        out_shape=jax.ShapeDtypeStruct((b, N_HEADS, HEAD_DIM), jnp.int32),
        compiler_params=_params(2),
        name="sample_moba_select",
    )(page_table, q, *([cache_k] * GATE_PAGES_PER_STEP))
    return sel[:, :, :MOBA_TOPK]


N_SEL_PAGES = GROUP * MOBA_TOPK * PAGES_PER_BLOCK


def _moba_sample_kernel(pt_ref, sel_ref, slopes_ref, q_ref, kn_ref, vn_ref, ck_ref, cv_ref, o_ref,
                        kbuf_ref, vbuf_ref, sem_ref, *, past_len, layer_a):
    b = pl.program_id(0)
    kvh = pl.program_id(1)
    n_kv = pl.num_programs(1)
    step = b * n_kv + kvh
    slot = step % 2

    def page_copies(seq, head_kv, into):
        copies = []
        for g in range(GROUP):
            for t in range(MOBA_TOPK):
                block = sel_ref[(seq * N_HEADS + head_kv * GROUP + g) * MOBA_TOPK + t]
                for r in range(PAGES_PER_BLOCK):
                    page = pt_ref[seq, block * PAGES_PER_BLOCK + r]
                    idx = (g * MOBA_TOPK + t) * PAGES_PER_BLOCK + r
                    copies.append(pltpu.make_async_copy(
                        ck_ref.at[layer_a, page, :, head_kv, :], kbuf_ref.at[into, idx],
                        sem_ref.at[into, 0]))
                    copies.append(pltpu.make_async_copy(
                        cv_ref.at[layer_a, page, :, head_kv, :], vbuf_ref.at[into, idx],
                        sem_ref.at[into, 1]))
        return copies

    @pl.when(step == 0)
    def _():
        for copy in page_copies(b, kvh, slot):
            copy.start()

    @pl.when(step + 1 < pl.num_programs(0) * n_kv)
    def _():
        for copy in page_copies((step + 1) // n_kv, (step + 1) % n_kv, 1 - slot):
            copy.start()

    for copy in page_copies(b, kvh, slot):
        copy.wait()

    within = lax.broadcasted_iota(jnp.int32, (PAGE_SIZE, 1), 0)
    for g in range(GROUP):
        head = kvh * GROUP + g
        slope = slopes_ref[head]
        q_row = q_ref[g:g + 1, :]
        s_own = _row_scores(kn_ref[...], q_row)
        scores = []
        m = s_own
        for t in range(MOBA_TOPK):
            first = sel_ref[(b * N_HEADS + head) * MOBA_TOPK + t] * MOBA_BLOCK
            for r in range(PAGES_PER_BLOCK):
                idx = (g * MOBA_TOPK + t) * PAGES_PER_BLOCK + r
                dist = (past_len - (first + r * PAGE_SIZE) - within).astype(F32)
                s = _row_scores(kbuf_ref[slot, idx], q_row) - slope * dist
                scores.append(s)
                m = jnp.maximum(m, jnp.max(s, axis=0, keepdims=True))
        p_own = jnp.exp(s_own - m)
        l = p_own
        o = p_own * vn_ref[...]
        for n, s in enumerate(scores):
            p = jnp.exp(s - m)
            l = l + jnp.sum(p, axis=0, keepdims=True)
            v = vbuf_ref[slot, g * MOBA_TOPK * PAGES_PER_BLOCK + n]
            o = o + jnp.sum(p * v, axis=0, keepdims=True)
        o_ref[g:g + 1, :] = o / l


def sample_moba_attention(q, k_new, v_new, cache_k, cache_v, page_table, sel, layer_a):
    b = q.shape[0]
    past_len = page_table.shape[1] * PAGE_SIZE
    q_spec = pl.BlockSpec((None, None, GROUP, HEAD_DIM), lambda i, h, pt, sl: (i, h, 0, 0))
    row_spec = pl.BlockSpec((None, None, 1, HEAD_DIM), lambda i, h, pt, sl: (i, h, 0, 0))
    in_hbm = pl.BlockSpec(memory_space=pl.ANY)
    page_buffers = pltpu.VMEM((2, N_SEL_PAGES, PAGE_SIZE, HEAD_DIM), F32)
    grid_spec = pltpu.PrefetchScalarGridSpec(
        num_scalar_prefetch=2,
        grid=(b, N_KV_HEADS),
        in_specs=[pl.BlockSpec(memory_space=pltpu.SMEM), q_spec, row_spec, row_spec,
                  in_hbm, in_hbm],
        out_specs=q_spec,
        scratch_shapes=[page_buffers, page_buffers, pltpu.SemaphoreType.DMA((2, 2))],
    )
    return pl.pallas_call(
        functools.partial(_moba_sample_kernel, past_len=past_len, layer_a=layer_a),
        grid_spec=grid_spec,
        out_shape=jax.ShapeDtypeStruct((b, N_KV_HEADS, GROUP, HEAD_DIM), F32),
        compiler_params=_params(2),
        name="sample_moba",
    )(page_table, sel.reshape(-1), jnp.asarray(SLOPES), q, k_new, v_new, cache_k, cache_v)


DENSE_TM = 1024
DOWN_TM = 512


def _half_ffn(x, xg, ssq, gain, wg_all, wu_all, wd_all, layer, next_gain):
    a, a_tail = ffn_up(xg, ssq, x, gain, wg_all, wu_all, layer, tm=DENSE_TM, tn=512)
    return matmul_residual([a], [a_tail], wd_all, layer, x, 0.5, tm=DOWN_TM, tn=512,
                           next_gain=next_gain)


def kernel(x_prompt, x_sample, cache_moba_k, cache_moba_v, cache_swa_k, cache_swa_v, cache_mem_k, cache_mem_v, page_table, mem_prompt, g_ffn1, w_ffn1_gate, w_ffn1_up, w_ffn1_down, g_attn, w_in, w_out, sinks, g_mem, w_mem_kv, g_ffn2, w_ffn2_gate, w_ffn2_up, w_ffn2_down, g_final):
    bp, seq, d = x_prompt.shape
    bs = x_sample.shape[0]
    m = bp * seq
    n_pool = cache_moba_k.shape[1]
    mem = mem_prompt.reshape(bp * N_MEM, d)
    mem_k_cache = cache_mem_k.reshape(DEPTH, bs, N_MEM * N_MEM_HEADS, HEAD_DIM)
    mem_v_cache = cache_mem_v.reshape(DEPTH, bs, N_MEM * N_MEM_HEADS, HEAD_DIM)
    swa_k_cache = cache_swa_k.reshape(-1, bs, WINDOW * N_KV_HEADS, HEAD_DIM)
    swa_v_cache = cache_swa_v.reshape(-1, bs, WINDOW * N_KV_HEADS, HEAD_DIM)
    moba_k_flat = cache_moba_k.reshape(-1, n_pool, PAGE_ROWS, HEAD_DIM)

    moba_qkv, moba_ks, moba_vs = [], [], []
    swa_kp, swa_vp, swa_ks, swa_vs = [], [], [], []
    mem_kp, mem_vp = [], []
    x, xg, ssq = stream_start(x_prompt.reshape(m, d), x_sample.reshape(bs, d), g_ffn1, 0,
                              tm=DOWN_TM)
    for l in range(DEPTH):
        x, xg, ssq = _half_ffn(x, xg, ssq, (g_ffn1, l), w_ffn1_gate, w_ffn1_up, w_ffn1_down, l,
                               (g_attn, l))

        qkv = norm_project(xg, ssq, x, (g_attn, l), w_in, l, tm=DENSE_TM, tn=1024)
        qkv_s = qkv[m:]
        memkv = project(rms_norm(mem, g_mem, l, BF16), w_mem_kv, l, tn=512)
        mem_kp.append(memkv[:, :MQ_W].reshape(bp, N_MEM, N_MEM_HEADS, HEAD_DIM))
        mem_vp.append(memkv[:, MQ_W:].reshape(bp, N_MEM, N_MEM_HEADS, HEAD_DIM))

        qs = qkv_s[:, :Q_W]
        ks = qkv_s[:, Q_W:Q_W + KV_W]
        vs = qkv_s[:, Q_W + KV_W:Q_W + 2 * KV_W]
        qms = qkv_s[:, Q_W + 2 * KV_W:]
        ks4 = ks.reshape(bs, 1, N_KV_HEADS, HEAD_DIM)
        vs4 = vs.reshape(bs, 1, N_KV_HEADS, HEAD_DIM)

        oms = sample_memory_attention(qms.reshape(bs, 1, MQ_W), mem_k_cache, mem_v_cache, l)
        j = l // 2
        if l % 2 == 0:
            osp, omp = prompt_attention(qkv, memkv, bp, seq, None)
            sel = sample_moba_select(qs.reshape(bs, N_HEADS, HEAD_DIM), moba_k_flat, page_table, j)
            oss = sample_moba_attention(
                qs.reshape(bs, N_KV_HEADS, GROUP, HEAD_DIM),
                ks.reshape(bs, N_KV_HEADS, 1, HEAD_DIM), vs.reshape(bs, N_KV_HEADS, 1, HEAD_DIM),
                cache_moba_k, cache_moba_v, page_table, sel, j)
            moba_qkv.append(qkv)
            moba_ks.append(ks4)
            moba_vs.append(vs4)
        else:
            osp, omp = prompt_attention(qkv, memkv, bp, seq, sinks[j])
            oss = sample_swa_attention(qs.reshape(bs, 1, Q_W), ks.reshape(bs, 1, KV_W),
                                       vs.reshape(bs, 1, KV_W), swa_k_cache, swa_v_cache,
                                       sinks[j], j)
            last = jnp.stack([qkv[(b + 1) * seq - WINDOW:(b + 1) * seq, Q_W:Q_W + 2 * KV_W]
                              for b in range(bp)])
            swa_kp.append(last[:, :, :KV_W].reshape(bp, WINDOW, N_KV_HEADS, HEAD_DIM))
            swa_vp.append(last[:, :, KV_W:].reshape(bp, WINDOW, N_KV_HEADS, HEAD_DIM))
            swa_ks.append(jnp.concatenate([cache_swa_k[j], ks4], axis=1)[:, -WINDOW:])
            swa_vs.append(jnp.concatenate([cache_swa_v[j], vs4], axis=1)[:, -WINDOW:])

        x, xg, ssq = matmul_residual(
            [osp, omp], [oss.reshape(bs, Q_W), oms.reshape(bs, MQ_W)], w_out, l, x, 1.0,
            tm=DENSE_TM, tn=1024, next_gain=(g_ffn2, l))

        next_gain = (g_ffn1, l + 1) if l + 1 < DEPTH else None
        out = _half_ffn(x, xg, ssq, (g_ffn2, l), w_ffn2_gate, w_ffn2_up, w_ffn2_down, l,
                        next_gain)
        if next_gain is None:
            (x,) = out
        else:
            x, xg, ssq = out

    y_prompt, y_sample = stream_end(x, g_final, m, tm=DOWN_TM)
    moba_kp, moba_vp = export_prompt_kv(moba_qkv, m, tm=DOWN_TM)
    kv_shape = (len(moba_qkv), bp, seq, N_KV_HEADS, HEAD_DIM)
    return (y_prompt.reshape(bp, seq, d), y_sample.reshape(bs, 1, d),
            moba_kp.reshape(kv_shape), moba_vp.reshape(kv_shape),
            jnp.stack(moba_ks), jnp.stack(moba_vs),
            jnp.stack(swa_kp), jnp.stack(swa_vp), jnp.stack(swa_ks), jnp.stack(swa_vs),
            jnp.stack(mem_kp), jnp.stack(mem_vp))
```

```python
import functools
import math

import jax
import jax.numpy as jnp
import numpy as np
from jax import lax
from jax.experimental import pallas as pl
from jax.experimental.pallas import tpu as pltpu

F32 = jnp.float32
BF16 = jnp.bfloat16

D_MODEL = 2048
DEPTH = 4
PAGE_SIZE = 128
HEAD_DIM = 128
N_HEADS = 12
N_KV_HEADS = 4
GROUP = N_HEADS // N_KV_HEADS
N_MEM_HEADS = 4
N_MEM = 256
D_FF = 5632
MOBA_BLOCK = 256
MOBA_TOPK = 3
WINDOW = 128
RMS_EPS = 1e-6
Q_W = N_HEADS * HEAD_DIM
KV_W = N_KV_HEADS * HEAD_DIM
MQ_W = N_MEM_HEADS * HEAD_DIM
IN_W = Q_W + 2 * KV_W + MQ_W
SCALE = HEAD_DIM ** -0.5
PAGES_PER_BLOCK = MOBA_BLOCK // PAGE_SIZE

VMEM_LIMIT_BYTES = 56 * 1024 * 1024
MASKED = -1e30
Q_TILE = 256
GATE_PAGES_PER_STEP = 64


def _alibi_slope_list(n):
    def pow2(m):
        start = 2.0 ** (-(2.0 ** -(math.log2(m) - 3)))
        return [start ** (i + 1) for i in range(m)]
    if math.log2(n).is_integer():
        return pow2(n)
    c = 2 ** math.floor(math.log2(n))
    return pow2(c) + _alibi_slope_list(2 * c)[0::2][:n - c]


SLOPES = np.array(_alibi_slope_list(N_HEADS), dtype=np.float32)


def _params(n_grid_dims):
    return pltpu.CompilerParams(
        dimension_semantics=("arbitrary",) * n_grid_dims,
        vmem_limit_bytes=VMEM_LIMIT_BYTES)


def _dot_t(a, b, precision=None):
    return lax.dot_general(a, b, (((1,), (1,)), ((), ())), precision=precision,
                           preferred_element_type=F32)


def _norm_kernel(x_ref, g_ref, o_ref):
    x = x_ref[...]
    ms = jnp.mean(x * x, axis=-1, keepdims=True)
    o_ref[...] = (x * lax.rsqrt(ms + RMS_EPS) * g_ref[...]).astype(o_ref.dtype)


def rms_norm(x, g_all, layer, out_dtype):
    m, d = x.shape
    tm = min(m, 512)
    g3 = g_all.reshape(g_all.shape[0], 1, d)
    return pl.pallas_call(
        _norm_kernel,
        grid=(m // tm,),
        in_specs=[pl.BlockSpec((tm, d), lambda i: (i, 0)),
                  pl.BlockSpec((None, 1, d), lambda i: (layer, 0, 0))],
        out_specs=pl.BlockSpec((tm, d), lambda i: (i, 0)),
        out_shape=jax.ShapeDtypeStruct((m, d), out_dtype),
        compiler_params=_params(1),
        name="rms_norm",
    )(x, g3)


def _proj_kernel(h_ref, w_ref, o_ref, wb_ref):
    @pl.when(pl.program_id(1) == 0)
    def _():
        wb_ref[...] = w_ref[...].astype(BF16)
    o_ref[...] = jnp.dot(h_ref[...], wb_ref[...], preferred_element_type=F32)


def project(h, w_all, layer, tn):
    m, k = h.shape
    n = w_all.shape[2]
    tm = min(m, 1024)
    return pl.pallas_call(
        _proj_kernel,
        grid=(n // tn, m // tm),
        in_specs=[pl.BlockSpec((tm, k), lambda j, i: (i, 0)),
                  pl.BlockSpec((None, k, tn), lambda j, i: (layer, 0, j))],
        out_specs=pl.BlockSpec((tm, tn), lambda j, i: (i, j)),
        out_shape=jax.ShapeDtypeStruct((m, n), F32),
        scratch_shapes=[pltpu.VMEM((k, tn), BF16)],
        compiler_params=_params(2),
        name="project",
    )(h, w_all)


LANES = 128


def _tail_first(i, n_full):
    return jnp.where(i == 0, n_full, i - 1)


def _main_tile(i):
    return jnp.maximum(i - 1, 0)


def _lane_partial_sums(x):
    total = x[:, :LANES]
    for c in range(1, x.shape[1] // LANES):
        total = total + x[:, c * LANES:(c + 1) * LANES]
    return total


def _store_norm_inputs(x, g_ref, xg_ref, ssq_ref):
    xg_ref[...] = (x * g_ref[...]).astype(xg_ref.dtype)
    ssq_ref[...] = _lane_partial_sums(x * x)


def _inv_rms(ssq, d_model):
    total = jnp.sum(jnp.sum(ssq, axis=0), axis=-1, keepdims=True)
    return lax.rsqrt(total * (1.0 / d_model) + RMS_EPS)


def _normed_rows(x, g):
    return x * lax.rsqrt(jnp.mean(x * x, axis=-1, keepdims=True) + RMS_EPS) * g


def _dot_split(a, w, w_hi):
    a_hi = a.astype(BF16)
    a_lo = (a - a_hi.astype(F32)).astype(BF16)
    w_lo = (w - w_hi.astype(F32)).astype(BF16)
    t = a.shape[0]
    on_hi = jnp.dot(jnp.concatenate([a_hi, a_lo], axis=0), w_hi, preferred_element_type=F32)
    return on_hi[:t] + on_hi[t:] + jnp.dot(a_hi, w_lo, preferred_element_type=F32)


def _stream_start_kernel(xp_ref, xs_ref, g_ref, x_ref, xg_ref, ssq_ref, *, tail):
    i = pl.program_id(0)

    @pl.when(i == 0)
    def _():
        x_ref[0:tail, :] = xs_ref[...]

    @pl.when(i > 0)
    def _():
        x = xp_ref[...]
        x_ref[...] = x
        _store_norm_inputs(x, g_ref, xg_ref, ssq_ref)


def stream_start(x_main, x_tail, g_all, layer, tm):
    m, d = x_main.shape
    tail = x_tail.shape[0]
    n_full = m // tm
    g3 = g_all.reshape(g_all.shape[0], 1, d)
    main_block = lambda i: (_main_tile(i), 0)
    return pl.pallas_call(
        functools.partial(_stream_start_kernel, tail=tail),
        grid=(n_full + 1,),
        in_specs=[pl.BlockSpec((tm, d), main_block),
                  pl.BlockSpec((tail, d), lambda i: (0, 0)),
                  pl.BlockSpec((None, 1, d), lambda i: (layer, 0, 0))],
        out_specs=[pl.BlockSpec((tm, d), lambda i: (_tail_first(i, n_full), 0)),
                   pl.BlockSpec((tm, d), main_block),
                   pl.BlockSpec((None, tm, LANES), lambda i: (0, _main_tile(i), 0))],
        out_shape=[jax.ShapeDtypeStruct((m + tail, d), F32), jax.ShapeDtypeStruct((m, d), BF16),
                   jax.ShapeDtypeStruct((1, m, LANES), F32)],
        compiler_params=_params(1),
        name="stream_start",
    )(x_main, x_tail, g3)


def _stream_end_kernel(x_ref, g_ref, yp_ref, ys_ref, *, tail):
    i = pl.program_id(0)

    @pl.when(i == 0)
    def _():
        ys_ref[...] = _normed_rows(x_ref[0:tail, :], g_ref[...])

    @pl.when(i > 0)
    def _():
        yp_ref[...] = _normed_rows(x_ref[...], g_ref[...])


def stream_end(x, g, m, tm):
    rows, d = x.shape
    tail = rows - m
    n_full = m // tm
    return pl.pallas_call(
        functools.partial(_stream_end_kernel, tail=tail),
        grid=(n_full + 1,),
        in_specs=[pl.BlockSpec((tm, d), lambda i: (_tail_first(i, n_full), 0)),
                  pl.BlockSpec((1, d), lambda i: (0, 0))],
        out_specs=[pl.BlockSpec((tm, d), lambda i: (_main_tile(i), 0)),
                   pl.BlockSpec((tail, d), lambda i: (0, 0))],
        out_shape=[jax.ShapeDtypeStruct((m, d), F32), jax.ShapeDtypeStruct((tail, d), F32)],
        compiler_params=_params(1),
        name="stream_end",
    )(x, g.reshape(1, d))


def _normed_specs(x, gain, nj, tm, n_full):
    rows, k = x.shape
    tail = rows - n_full * tm
    g_all, g_layer = gain
    specs = [pl.BlockSpec((tm, k), lambda j, i: (_main_tile(i), 0)),
             pl.BlockSpec((nj, tm, LANES),
                          lambda j, i: (0, jnp.where(j == 0, _main_tile(i), n_full - 1), 0)),
             pl.BlockSpec((tail, k), lambda j, i: (n_full * tm // tail, 0)),
             pl.BlockSpec((None, 1, k), lambda j, i: (g_layer, 0, 0))]
    return specs, [x, g_all.reshape(-1, 1, k)]


def _cached_inv_rms(ssq_ref, inv_ref, j, i, d_model):
    @pl.when(j == 0)
    def _():
        inv_ref[i] = _inv_rms(ssq_ref[...], d_model)
    return inv_ref[i]


def _norm_proj_kernel(xg_ref, ssq_ref, xt_ref, g_ref, w_ref, o_ref, wb_ref, inv_ref, *, tail):
    j = pl.program_id(0)
    i = pl.program_id(1)

    @pl.when(i == 0)
    def _():
        wb_ref[...] = w_ref[...].astype(BF16)
        h = _normed_rows(xt_ref[...], g_ref[...])
        o_ref[0:tail, :] = _dot_split(h, w_ref[...], wb_ref[...])

    @pl.when(i > 0)
    def _():
        inv = _cached_inv_rms(ssq_ref, inv_ref, j, i, xg_ref.shape[1])
        o_ref[...] = inv * jnp.dot(xg_ref[...], wb_ref[...], preferred_element_type=F32)


def norm_project(xg, ssq, x, gain, w_all, layer, tm, tn):
    m, k = xg.shape
    rows = x.shape[0]
    n = w_all.shape[2]
    n_full = m // tm
    specs, operands = _normed_specs(x, gain, ssq.shape[0], tm, n_full)
    return pl.pallas_call(
        functools.partial(_norm_proj_kernel, tail=rows - m),
        grid=(n // tn, n_full + 1),
        in_specs=specs + [pl.BlockSpec((None, k, tn), lambda j, i: (layer, 0, j))],
        out_specs=pl.BlockSpec((tm, tn), lambda j, i: (_tail_first(i, n_full), j)),
        out_shape=jax.ShapeDtypeStruct((rows, n), F32),
        scratch_shapes=[pltpu.VMEM((k, tn), BF16), pltpu.VMEM((n_full + 1, tm, 1), F32)],
        compiler_params=_params(2),
        name="norm_project",
    )(xg, ssq, *operands, w_all)


def _ffn_up_kernel(xg_ref, ssq_ref, xt_ref, g_ref, wg_ref, wu_ref, a_ref, at_ref,
                   wgb_ref, wub_ref, inv_ref):
    j = pl.program_id(0)
    i = pl.program_id(1)

    def swiglu(g, u):
        return g * jax.nn.sigmoid(g) * u

    @pl.when(i == 0)
    def _():
        wgb_ref[...] = wg_ref[...].astype(BF16)
        wub_ref[...] = wu_ref[...].astype(BF16)
        h = _normed_rows(xt_ref[...], g_ref[...])
        at_ref[...] = swiglu(_dot_split(h, wg_ref[...], wgb_ref[...]),
                             _dot_split(h, wu_ref[...], wub_ref[...]))

    @pl.when(i > 0)
    def _():
        inv = _cached_inv_rms(ssq_ref, inv_ref, j, i, xg_ref.shape[1])
        h = xg_ref[...]
        g = inv * jnp.dot(h, wgb_ref[...], preferred_element_type=F32)
        u = inv * jnp.dot(h, wub_ref[...], preferred_element_type=F32)
        a_ref[...] = swiglu(g, u).astype(a_ref.dtype)


def ffn_up(xg, ssq, x, gain, wg_all, wu_all, layer, tm, tn):
    m, k = xg.shape
    tail = x.shape[0] - m
    n = wg_all.shape[2]
    n_full = m // tm
    specs, operands = _normed_specs(x, gain, ssq.shape[0], tm, n_full)
    w_spec = pl.BlockSpec((None, k, tn), lambda j, i: (layer, 0, j))
    return pl.pallas_call(
        _ffn_up_kernel,
        grid=(n // tn, n_full + 1),
        in_specs=specs + [w_spec, w_spec],
        out_specs=[pl.BlockSpec((tm, tn), lambda j, i: (_main_tile(i), j)),
                   pl.BlockSpec((tail, tn), lambda j, i: (0, j))],
        out_shape=[jax.ShapeDtypeStruct((m, n), BF16), jax.ShapeDtypeStruct((tail, n), F32)],
        scratch_shapes=[pltpu.VMEM((k, tn), BF16), pltpu.VMEM((k, tn), BF16),
                        pltpu.VMEM((n_full + 1, tm, 1), F32)],
        compiler_params=_params(2),
        name="ffn_up",
    )(xg, ssq, *operands, wg_all, wu_all)


def _mm_res_kernel(*refs, k_sizes, emit_norm_inputs, scale, tail):
    n_a = len(k_sizes)
    main_refs = refs[:n_a]
    tail_refs = refs[n_a:2 * n_a]
    if emit_norm_inputs:
        w_ref, x_ref, g_ref, o_ref, xg_ref, ssq_ref, wb_ref = refs[2 * n_a:]
    else:
        w_ref, x_ref, o_ref, wb_ref = refs[2 * n_a:]
    i = pl.program_id(1)

    def product(a_refs, dot):
        acc = None
        off = 0
        for a_ref, ksz in zip(a_refs, k_sizes):
            d = dot(a_ref[...], slice(off, off + ksz))
            acc = d if acc is None else acc + d
            off += ksz
        return acc

    @pl.when(i == 0)
    def _():
        wb_ref[...] = w_ref[...].astype(BF16)
        acc = product(tail_refs, lambda a, rows: _dot_split(a, w_ref[rows, :], wb_ref[rows, :]))
        o_ref[0:tail, :] = x_ref[0:tail, :] + scale * acc

    @pl.when(i > 0)
    def _():
        acc = product(main_refs, lambda a, rows: jnp.dot(a.astype(BF16), wb_ref[rows, :],
                                                         preferred_element_type=F32))
        x = x_ref[...] + scale * acc
        o_ref[...] = x
        if emit_norm_inputs:
            _store_norm_inputs(x, g_ref, xg_ref, ssq_ref)


def matmul_residual(a_main, a_tail, w_all, layer, x, scale, tm, tn, next_gain=None):
    rows, n = x.shape
    m = a_main[0].shape[0]
    tail = rows - m
    n_full = m // tm
    k_sizes = tuple(a.shape[1] for a in a_main)
    k = sum(k_sizes)
    nj = n // tn
    a_specs = [pl.BlockSpec((tm, ks), lambda j, i: (_main_tile(i), 0)) for ks in k_sizes]
    a_specs += [pl.BlockSpec((tail, ks), lambda j, i: (0, 0)) for ks in k_sizes]
    tile = pl.BlockSpec((tm, tn), lambda j, i: (_tail_first(i, n_full), j))
    in_specs = a_specs + [pl.BlockSpec((None, k, tn), lambda j, i: (layer, 0, j)), tile]
    operands = list(a_main) + list(a_tail) + [w_all, x]
    out_specs = [tile]
    out_shape = [jax.ShapeDtypeStruct((rows, n), F32)]
    emit = next_gain is not None
    if emit:
        g_all, g_layer = next_gain
        in_specs.append(pl.BlockSpec((None, 1, tn), lambda j, i: (g_layer, 0, j)))
        operands.append(g_all.reshape(-1, 1, n))
        out_specs += [pl.BlockSpec((tm, tn), lambda j, i: (_main_tile(i), j)),
                      pl.BlockSpec((None, tm, LANES), lambda j, i: (j, _main_tile(i), 0))]
        out_shape += [jax.ShapeDtypeStruct((m, n), BF16),
                      jax.ShapeDtypeStruct((nj, m, LANES), F32)]
    return pl.pallas_call(
        functools.partial(_mm_res_kernel, k_sizes=k_sizes, emit_norm_inputs=emit, scale=scale,
                          tail=tail),
        grid=(nj, n_full + 1),
        in_specs=in_specs, out_specs=out_specs, out_shape=out_shape,
        scratch_shapes=[pltpu.VMEM((k, tn), BF16)],
        compiler_params=_params(2),
        name="matmul_residual",
    )(*operands)


def _softmax_pv(s, v):
    m = jnp.max(s, axis=-1, keepdims=True)
    p = jnp.exp(s - m)
    l = jnp.sum(p, axis=-1, keepdims=True)
    return jnp.dot(p.astype(BF16), v, preferred_element_type=F32), m, l


def _memory_attention_tile(qm_ref, mk_ref, mv_ref, om_ref):
    qm = (qm_ref[...] * SCALE).astype(BF16)
    s = _dot_t(qm, mk_ref[...].astype(BF16))
    o, _, l = _softmax_pv(s, mv_ref[...].astype(BF16))
    om_ref[...] = (o / l).astype(om_ref.dtype)


def _moba_prompt_kernel(slopes_ref, q_ref, k_ref, v_ref, qm_ref, mk_ref, mv_ref,
                        os_ref, om_ref, kb_ref, vt_ref, kmean_ref, bias_ref, nslope_ref,
                        chosen_ref, sa_ref, sb_ref, acc_ref, m_ref, l_ref):
    kvh = pl.program_id(1)
    i = pl.program_id(2)
    nb = kmean_ref.shape[0]
    tq = q_ref.shape[0]
    width = GROUP * tq
    key = lax.broadcasted_iota(jnp.int32, (MOBA_BLOCK, width), 0)
    query = lax.broadcasted_iota(jnp.int32, (MOBA_BLOCK, width), 1) & (tq - 1)

    @pl.when(i == 0)
    def _():
        kb_ref[...] = k_ref[...].astype(BF16)
        for n in range(nb):
            rows = slice(n * MOBA_BLOCK, (n + 1) * MOBA_BLOCK)
            vt_ref[n] = v_ref[rows, :].T.astype(BF16)
            kmean_ref[n:n + 1, :] = jnp.mean(k_ref[rows, :], axis=0, keepdims=True)
        lane = lax.broadcasted_iota(jnp.int32, (1, width), 1)
        nslope = jnp.zeros((1, width), F32)
        for g in range(GROUP):
            nslope = jnp.where((lane >= g * tq) & (lane < (g + 1) * tq),
                               -slopes_ref[kvh * GROUP + g], nslope)
        nslope_ref[...] = nslope
        bias_ref[...] = nslope * (query - key).astype(F32)

    q_t = jnp.concatenate(
        [q_ref[:, g * HEAD_DIM:(g + 1) * HEAD_DIM] for g in range(GROUP)], axis=0).T
    gate = jnp.dot(kmean_ref[...], q_t, precision=lax.Precision.HIGHEST,
                   preferred_element_type=F32)
    blk = lax.broadcasted_iota(jnp.int32, (nb, width), 0)
    rank = jnp.zeros((nb, width), jnp.int32)
    for m in range(nb - 1):
        gm = gate[m:m + 1, :]
        beats = (gm > gate) | ((gm == gate) & (m < blk))
        rank = rank + jnp.where(beats & (m < i), 1, 0)
    chosen_ref[...] = jnp.where((blk < i) & (rank < MOBA_TOPK), 1.0, 0.0)

    qb_t = (q_t * SCALE).astype(BF16)

    def scores(block):
        start = pl.multiple_of(block * MOBA_BLOCK, MOBA_BLOCK)
        return jnp.dot(kb_ref[pl.ds(start, MOBA_BLOCK), :], qb_t, preferred_element_type=F32)

    sa_ref[...] = scores(0)
    s = jnp.where(query >= key, scores(i) + bias_ref[...], MASKED)
    m_own = jnp.max(s, axis=0, keepdims=True)
    p = jnp.exp(s - m_own)
    m_ref[...] = m_own
    l_ref[...] = jnp.sum(p, axis=0, keepdims=True)
    acc_ref[...] = jnp.dot(vt_ref[i], p.astype(BF16), preferred_element_type=F32)

    def absorb(s_ref, block):
        offset = ((i - block) * MOBA_BLOCK).astype(F32)
        shift = jnp.where(chosen_ref[pl.ds(block, 1), :] > 0.0, nslope_ref[...] * offset, MASKED)
        s = s_ref[...] + bias_ref[...] + shift
        m_old = m_ref[...]
        m_new = jnp.maximum(m_old, jnp.max(s, axis=0, keepdims=True))
        alpha = jnp.exp(m_old - m_new)
        p = jnp.exp(s - m_new)
        m_ref[...] = m_new
        l_ref[...] = alpha * l_ref[...] + jnp.sum(p, axis=0, keepdims=True)
        acc_ref[...] = alpha * acc_ref[...] + jnp.dot(vt_ref[block], p.astype(BF16),
                                                      preferred_element_type=F32)

    def past_pair(t, carry):
        first = 2 * t
        sb_ref[...] = scores(first + 1)
        absorb(sa_ref, first)
        sa_ref[...] = scores(jnp.minimum(first + 2, nb - 1))
        absorb(sb_ref, first + 1)
        return carry

    lax.fori_loop(0, i // 2, past_pair, 0)

    @pl.when(i % 2 == 1)
    def _():
        absorb(sa_ref, i - 1)

    out_t = acc_ref[...] / l_ref[...]
    for g in range(GROUP):
        os_ref[:, g * HEAD_DIM:(g + 1) * HEAD_DIM] = (
            out_t[:, g * tq:(g + 1) * tq].T.astype(os_ref.dtype))

    _memory_attention_tile(qm_ref, mk_ref, mv_ref, om_ref)


def _swa_prompt_kernel(slopes_ref, sinks_ref, q_ref, k_ref, v_ref, qm_ref, mk_ref, mv_ref,
                       os_ref, om_ref, kb_ref, vt_ref, bias_ref, sink_ref):
    kvh = pl.program_id(1)
    i = pl.program_id(2)
    tq = q_ref.shape[0]
    width = GROUP * tq
    n_chunks = tq // WINDOW + 1
    span = n_chunks * WINDOW

    @pl.when(i == 0)
    def _():
        kb_ref[...] = k_ref[...].astype(BF16)
        for n in range(vt_ref.shape[0]):
            vt_ref[n] = v_ref[n * WINDOW:(n + 1) * WINDOW, :].T.astype(BF16)
        lane = lax.broadcasted_iota(jnp.int32, (1, width), 1)
        nslope = jnp.zeros((1, width), F32)
        sink = jnp.zeros((1, width), F32)
        for g in range(GROUP):
            in_head = (lane >= g * tq) & (lane < (g + 1) * tq)
            nslope = jnp.where(in_head, -slopes_ref[kvh * GROUP + g], nslope)
            sink = jnp.where(in_head, sinks_ref[kvh * GROUP + g], sink)
        sink_ref[...] = sink
        key = lax.broadcasted_iota(jnp.int32, (span, width), 0)
        query = lax.broadcasted_iota(jnp.int32, (span, width), 1) & (tq - 1)
        for placement, lead in enumerate((0, WINDOW)):
            dist = query - key + lead
            bias_ref[placement] = jnp.where((dist >= 0) & (dist <= WINDOW),
                                            nslope * dist.astype(F32), MASKED)

    first_chunk = jnp.maximum(i * (tq // WINDOW) - 1, 0)
    kstart = pl.multiple_of(first_chunk * WINDOW, WINDOW)
    q_t = jnp.concatenate(
        [q_ref[:, g * HEAD_DIM:(g + 1) * HEAD_DIM] for g in range(GROUP)], axis=0).T
    qb_t = (q_t * SCALE).astype(BF16)
    s = (jnp.dot(kb_ref[pl.ds(kstart, span), :], qb_t, preferred_element_type=F32)
         + bias_ref[jnp.minimum(i, 1)])
    sink = sink_ref[...]
    m = jnp.maximum(jnp.max(s, axis=0, keepdims=True), sink)
    p = jnp.exp(s - m)
    l = jnp.sum(p, axis=0, keepdims=True) + jnp.exp(sink - m)
    pb = p.astype(BF16)
    out_t = None
    for c in range(n_chunks):
        part = jnp.dot(vt_ref[first_chunk + c], pb[c * WINDOW:(c + 1) * WINDOW, :],
                       preferred_element_type=F32)
        out_t = part if out_t is None else out_t + part
    out_t = out_t / l
    for g in range(GROUP):
        os_ref[:, g * HEAD_DIM:(g + 1) * HEAD_DIM] = (
            out_t[:, g * tq:(g + 1) * tq].T.astype(os_ref.dtype))

    _memory_attention_tile(qm_ref, mk_ref, mv_ref, om_ref)


def prompt_attention(qkv, memkv, batch, seq, sinks_layer):
    nq = seq // Q_TILE
    k_col = Q_W // HEAD_DIM
    v_col = (Q_W + KV_W) // HEAD_DIM
    qm_col = (Q_W + 2 * KV_W) // HEAD_DIM
    smem = pl.BlockSpec(memory_space=pltpu.SMEM)
    tensor_specs = [
        pl.BlockSpec((Q_TILE, GROUP * HEAD_DIM), lambda b, h, i: (b * nq + i, h)),
        pl.BlockSpec((seq, HEAD_DIM), lambda b, h, i: (b, k_col + h)),
        pl.BlockSpec((seq, HEAD_DIM), lambda b, h, i: (b, v_col + h)),
        pl.BlockSpec((Q_TILE, HEAD_DIM), lambda b, h, i: (b * nq + i, qm_col + h)),
        pl.BlockSpec((N_MEM, HEAD_DIM), lambda b, h, i: (b, h)),
        pl.BlockSpec((N_MEM, HEAD_DIM), lambda b, h, i: (b, N_MEM_HEADS + h)),
    ]
    out_specs = [
        pl.BlockSpec((Q_TILE, GROUP * HEAD_DIM), lambda b, h, i: (b * nq + i, h)),
        pl.BlockSpec((Q_TILE, HEAD_DIM), lambda b, h, i: (b * nq + i, h)),
    ]
    out_shape = [jax.ShapeDtypeStruct((batch * seq, Q_W), BF16),
                 jax.ShapeDtypeStruct((batch * seq, MQ_W), BF16)]
    width = GROUP * Q_TILE
    row_scratch = pltpu.VMEM((1, width), F32)
    slopes = jnp.asarray(SLOPES)
    tensors = (qkv, qkv, qkv, qkv, memkv, memkv)
    if sinks_layer is None:
        return pl.pallas_call(
            _moba_prompt_kernel,
            grid=(batch, N_KV_HEADS, nq),
            in_specs=[smem] + tensor_specs,
            out_specs=out_specs, out_shape=out_shape,
            scratch_shapes=[
                pltpu.VMEM((seq, HEAD_DIM), BF16),
                pltpu.VMEM((seq // MOBA_BLOCK, HEAD_DIM, MOBA_BLOCK), BF16),
                pltpu.VMEM((seq // MOBA_BLOCK, HEAD_DIM), F32),
                pltpu.VMEM((MOBA_BLOCK, width), F32),
                row_scratch,
                pltpu.VMEM((seq // MOBA_BLOCK, width), F32),
                pltpu.VMEM((MOBA_BLOCK, width), F32),
                pltpu.VMEM((MOBA_BLOCK, width), F32),
                pltpu.VMEM((HEAD_DIM, width), F32),
                row_scratch,
                row_scratch,
            ],
            compiler_params=_params(3),
            name="moba_prompt",
        )(slopes, *tensors)
    return pl.pallas_call(
        _swa_prompt_kernel,
        grid=(batch, N_KV_HEADS, nq),
        in_specs=[smem, smem] + tensor_specs,
        out_specs=out_specs, out_shape=out_shape,
        scratch_shapes=[
            pltpu.VMEM((seq, HEAD_DIM), BF16),
            pltpu.VMEM((seq // WINDOW, HEAD_DIM, WINDOW), BF16),
            pltpu.VMEM((2, Q_TILE + WINDOW, width), F32),
            row_scratch,
        ],
        compiler_params=_params(3),
        name="swa_prompt",
    )(slopes, sinks_layer, *tensors)


def _export_kv_kernel(k0_ref, v0_ref, k1_ref, v1_ref, ok_ref, ov_ref):
    def export(k_ref, v_ref):
        for h in range(N_KV_HEADS):
            lanes = slice(h * HEAD_DIM, (h + 1) * HEAD_DIM)
            rows = pl.ds(h, k_ref.shape[0], stride=N_KV_HEADS)
            ok_ref[rows, :] = k_ref[:, lanes]
            ov_ref[rows, :] = v_ref[:, lanes]

    @pl.when(pl.program_id(0) == 0)
    def _():
        export(k0_ref, v0_ref)

    @pl.when(pl.program_id(0) == 1)
    def _():
        export(k1_ref, v1_ref)


def export_prompt_kv(qkv_pair, m, tm):
    k_col = Q_W // KV_W
    v_col = k_col + 1

    def spec(layer, col):
        return pl.BlockSpec((tm, KV_W), lambda a, i: (jnp.where(a == layer, i, 0), col))

    out_spec = pl.BlockSpec((None, tm * N_KV_HEADS, HEAD_DIM), lambda a, i: (a, i, 0))
    out = jax.ShapeDtypeStruct((2, m * N_KV_HEADS, HEAD_DIM), F32)
    q0, q1 = qkv_pair
    return pl.pallas_call(
        _export_kv_kernel,
        grid=(2, m // tm),
        in_specs=[spec(0, k_col), spec(0, v_col), spec(1, k_col), spec(1, v_col)],
        out_specs=[out_spec, out_spec], out_shape=[out, out],
        compiler_params=_params(2),
        name="export_prompt_kv",
    )(q0, q0, q1, q1)


def _head_rows(ref, head, n_tokens, n_heads):
    return ref[pl.ds(head, n_tokens, stride=n_heads), :]


def _row_scores(k, q_row):
    return jnp.sum(k * q_row, axis=-1, keepdims=True) * SCALE


def _sample_memory_kernel(qm_ref, mk_ref, mv_ref, o_ref):
    for h in range(N_MEM_HEADS):
        lanes = slice(h * HEAD_DIM, (h + 1) * HEAD_DIM)
        s = _row_scores(_head_rows(mk_ref, h, N_MEM, N_MEM_HEADS), qm_ref[:, lanes])
        p = jnp.exp(s - jnp.max(s, axis=0, keepdims=True))
        l = jnp.sum(p, axis=0, keepdims=True)
        v = _head_rows(mv_ref, h, N_MEM, N_MEM_HEADS)
        o_ref[:, lanes] = jnp.sum(p * v, axis=0, keepdims=True) / l


def sample_memory_attention(qm, mem_k, mem_v, layer):
    b = qm.shape[0]
    row = pl.BlockSpec((None, 1, MQ_W), lambda i: (i, 0, 0))
    cache = pl.BlockSpec((None, None, N_MEM * N_MEM_HEADS, HEAD_DIM), lambda i: (layer, i, 0, 0))
    return pl.pallas_call(
        _sample_memory_kernel,
        grid=(b,),
        in_specs=[row, cache, cache],
        out_specs=row,
        out_shape=jax.ShapeDtypeStruct((b, 1, MQ_W), F32),
        compiler_params=_params(1),
        name="sample_memory",
    )(qm, mem_k, mem_v)


def _sample_swa_kernel(sinks_ref, q_ref, kc_ref, vc_ref, kn_ref, vn_ref, o_ref):
    back = (WINDOW - lax.broadcasted_iota(jnp.int32, (WINDOW, 1), 0)).astype(F32)
    for hq in range(N_HEADS):
        kv = hq // GROUP
        q_row = q_ref[:, hq * HEAD_DIM:(hq + 1) * HEAD_DIM]
        lanes = slice(kv * HEAD_DIM, (kv + 1) * HEAD_DIM)
        sink = sinks_ref[hq]
        s_c = (_row_scores(_head_rows(kc_ref, kv, WINDOW, N_KV_HEADS), q_row)
               - float(SLOPES[hq]) * back)
        s_n = _row_scores(kn_ref[:, lanes], q_row)
        m = jnp.maximum(jnp.maximum(jnp.max(s_c, axis=0, keepdims=True), s_n), sink)
        p_c = jnp.exp(s_c - m)
        p_n = jnp.exp(s_n - m)
        l = jnp.sum(p_c, axis=0, keepdims=True) + p_n + jnp.exp(sink - m)
        v_c = _head_rows(vc_ref, kv, WINDOW, N_KV_HEADS)
        o = jnp.sum(p_c * v_c, axis=0, keepdims=True) + p_n * vn_ref[:, lanes]
        o_ref[:, hq * HEAD_DIM:(hq + 1) * HEAD_DIM] = o / l


def sample_swa_attention(q, k_new, v_new, cache_k, cache_v, sinks_layer, layer_b):
    b = q.shape[0]
    q_spec = pl.BlockSpec((None, 1, Q_W), lambda i: (i, 0, 0))
    kv_row = pl.BlockSpec((None, 1, KV_W), lambda i: (i, 0, 0))
    cache = pl.BlockSpec((None, None, WINDOW * N_KV_HEADS, HEAD_DIM),
                         lambda i: (layer_b, i, 0, 0))
    return pl.pallas_call(
        _sample_swa_kernel,
        grid=(b,),
        in_specs=[pl.BlockSpec(memory_space=pltpu.SMEM), q_spec, cache, cache, kv_row, kv_row],
        out_specs=q_spec,
        out_shape=jax.ShapeDtypeStruct((b, 1, Q_W), F32),
        compiler_params=_params(1),
        name="sample_swa",
    )(sinks_layer, q, cache_k, cache_v, k_new, v_new)


PAGE_ROWS = PAGE_SIZE * N_KV_HEADS
SUBLANES = 8


def _moba_gate_kernel(pt_ref, q_ref, *refs):
    del pt_ref
    pages = refs[:GATE_PAGES_PER_STEP]
    sel_ref, kmean_ref = refs[GATE_PAGES_PER_STEP:]
    s = pl.program_id(1)
    blocks_per_step = GATE_PAGES_PER_STEP // PAGES_PER_BLOCK
    n_blocks = kmean_ref.shape[1]

    sub = lax.broadcasted_iota(jnp.int32, (blocks_per_step, HEAD_DIM), 0)
    means = [jnp.zeros((blocks_per_step, HEAD_DIM), F32) for _ in range(N_KV_HEADS)]
    for r in range(blocks_per_step):
        total = None
        for page in pages[r * PAGES_PER_BLOCK:(r + 1) * PAGES_PER_BLOCK]:
            part = jnp.sum(page[...].reshape(PAGE_ROWS // SUBLANES, SUBLANES, HEAD_DIM), axis=0)
            total = part if total is None else total + part
        for kv in range(N_KV_HEADS):
            head_sum = total[kv:kv + 1, :] + total[kv + N_KV_HEADS:kv + N_KV_HEADS + 1, :]
            means[kv] = jnp.where(sub == r, head_sum * (1.0 / MOBA_BLOCK), means[kv])
    first = pl.multiple_of(s * blocks_per_step, blocks_per_step)
    for kv in range(N_KV_HEADS):
        kmean_ref[kv, pl.ds(first, blocks_per_step), :] = means[kv]

    @pl.when(s == pl.num_programs(1) - 1)
    def _():
        q = q_ref[...]
        head = lax.broadcasted_iota(jnp.int32, (N_HEADS, n_blocks), 0)
        gate = jnp.zeros((N_HEADS, n_blocks), F32)
        for kv in range(N_KV_HEADS):
            g_kv = _dot_t(q, kmean_ref[kv], precision=lax.Precision.HIGHEST)
            gate = jnp.where((head >= kv * GROUP) & (head < (kv + 1) * GROUP), g_kv, gate)
        blk = lax.broadcasted_iota(jnp.int32, (N_HEADS, n_blocks), 1).astype(F32)
        lane = lax.broadcasted_iota(jnp.int32, sel_ref.shape, 1)
        picks = jnp.zeros(sel_ref.shape, F32)
        for t in range(MOBA_TOPK):
            best = jnp.max(gate, axis=-1, keepdims=True)
            idx = jnp.min(jnp.where(gate == best, blk, float(n_blocks)), axis=-1, keepdims=True)
            picks = jnp.where(lane == t, idx, picks)
            gate = jnp.where(blk == idx, -jnp.inf, gate)
        sel_ref[...] = picks.astype(jnp.int32)


def sample_moba_select(q, cache_k, page_table, layer_a):
    b = q.shape[0]
    n_pages = page_table.shape[1]
    n_blocks = n_pages // PAGES_PER_BLOCK
    steps = n_pages // GATE_PAGES_PER_STEP

    def page_spec(r):
        return pl.BlockSpec(
            (None, None, PAGE_ROWS, HEAD_DIM),
            lambda i, s, pt: (layer_a, pt[i, s * GATE_PAGES_PER_STEP + r], 0, 0))

    grid_spec = pltpu.PrefetchScalarGridSpec(
        num_scalar_prefetch=1,
        grid=(b, steps),
        in_specs=[pl.BlockSpec((None, N_HEADS, HEAD_DIM), lambda i, s, pt: (i, 0, 0))]
        + [page_spec(r) for r in range(GATE_PAGES_PER_STEP)],
        out_specs=pl.BlockSpec((None, N_HEADS, HEAD_DIM), lambda i, s, pt: (i, 0, 0)),
        scratch_shapes=[pltpu.VMEM((N_KV_HEADS, n_blocks, HEAD_DIM), F32)],
    )
    sel = pl.pallas_call(
        _moba_gate_kernel,
        grid_spec=grid_spec,
        out_shape=jax.ShapeDtypeStruct((b, N_HEADS, HEAD_DIM), jnp.int32),
        compiler_params=_params(2),
        name="sample_moba_select",
    )(page_table, q, *([cache_k] * GATE_PAGES_PER_STEP))
    return sel[:, :, :MOBA_TOPK]


N_SEL_PAGES = GROUP * MOBA_TOPK * PAGES_PER_BLOCK


def _moba_sample_kernel(pt_ref, sel_ref, slopes_ref, q_ref, kn_ref, vn_ref, ck_ref, cv_ref, o_ref,
                        kbuf_ref, vbuf_ref, sem_ref, *, past_len, layer_a):
    b = pl.program_id(0)
    kvh = pl.program_id(1)
    n_kv = pl.num_programs(1)
    step = b * n_kv + kvh
    slot = step % 2

    def page_copies(seq, head_kv, into):
        copies = []
        for g in range(GROUP):
            for t in range(MOBA_TOPK):
                block = sel_ref[(seq * N_HEADS + head_kv * GROUP + g) * MOBA_TOPK + t]
                for r in range(PAGES_PER_BLOCK):
                    page = pt_ref[seq, block * PAGES_PER_BLOCK + r]
                    idx = (g * MOBA_TOPK + t) * PAGES_PER_BLOCK + r
                    copies.append(pltpu.make_async_copy(
                        ck_ref.at[layer_a, page, :, head_kv, :], kbuf_ref.at[into, idx],
                        sem_ref.at[into, 0]))
                    copies.append(pltpu.make_async_copy(
                        cv_ref.at[layer_a, page, :, head_kv, :], vbuf_ref.at[into, idx],
                        sem_ref.at[into, 1]))
        return copies

    @pl.when(step == 0)
    def _():
        for copy in page_copies(b, kvh, slot):
            copy.start()

    @pl.when(step + 1 < pl.num_programs(0) * n_kv)
    def _():
        for copy in page_copies((step + 1) // n_kv, (step + 1) % n_kv, 1 - slot):
            copy.start()

    for copy in page_copies(b, kvh, slot):
        copy.wait()

    within = lax.broadcasted_iota(jnp.int32, (PAGE_SIZE, 1), 0)
    for g in range(GROUP):
        head = kvh * GROUP + g
        slope = slopes_ref[head]
        q_row = q_ref[g:g + 1, :]
        s_own = _row_scores(kn_ref[...], q_row)
        scores = []
        m = s_own
        for t in range(MOBA_TOPK):
            first = sel_ref[(b * N_HEADS + head) * MOBA_TOPK + t] * MOBA_BLOCK
            for r in range(PAGES_PER_BLOCK):
                idx = (g * MOBA_TOPK + t) * PAGES_PER_BLOCK + r
                dist = (past_len - (first + r * PAGE_SIZE) - within).astype(F32)
                s = _row_scores(kbuf_ref[slot, idx], q_row) - slope * dist
                scores.append(s)
                m = jnp.maximum(m, jnp.max(s, axis=0, keepdims=True))
        p_own = jnp.exp(s_own - m)
        l = p_own
        o = p_own * vn_ref[...]
        for n, s in enumerate(scores):
            p = jnp.exp(s - m)
            l = l + jnp.sum(p, axis=0, keepdims=True)
            v = vbuf_ref[slot, g * MOBA_TOPK * PAGES_PER_BLOCK + n]
            o = o + jnp.sum(p * v, axis=0, keepdims=True)
        o_ref[g:g + 1, :] = o / l


def sample_moba_attention(q, k_new, v_new, cache_k, cache_v, page_table, sel, layer_a):
    b = q.shape[0]
    past_len = page_table.shape[1] * PAGE_SIZE
    q_spec = pl.BlockSpec((None, None, GROUP, HEAD_DIM), lambda i, h, pt, sl: (i, h, 0, 0))
    row_spec = pl.BlockSpec((None, None, 1, HEAD_DIM), lambda i, h, pt, sl: (i, h, 0, 0))
    in_hbm = pl.BlockSpec(memory_space=pl.ANY)
    page_buffers = pltpu.VMEM((2, N_SEL_PAGES, PAGE_SIZE, HEAD_DIM), F32)
    grid_spec = pltpu.PrefetchScalarGridSpec(
        num_scalar_prefetch=2,
        grid=(b, N_KV_HEADS),
        in_specs=[pl.BlockSpec(memory_space=pltpu.SMEM), q_spec, row_spec, row_spec,
                  in_hbm, in_hbm],
        out_specs=q_spec,
        scratch_shapes=[page_buffers, page_buffers, pltpu.SemaphoreType.DMA((2, 2))],
    )
    return pl.pallas_call(
        functools.partial(_moba_sample_kernel, past_len=past_len, layer_a=layer_a),
        grid_spec=grid_spec,
        out_shape=jax.ShapeDtypeStruct((b, N_KV_HEADS, GROUP, HEAD_DIM), F32),
        compiler_params=_params(2),
        name="sample_moba",
    )(page_table, sel.reshape(-1), jnp.asarray(SLOPES), q, k_new, v_new, cache_k, cache_v)


DENSE_TM = 1024
DOWN_TM = 512


def _half_ffn(x, xg, ssq, gain, wg_all, wu_all, wd_all, layer, next_gain):
    a, a_tail = ffn_up(xg, ssq, x, gain, wg_all, wu_all, layer, tm=DENSE_TM, tn=512)
    return matmul_residual([a], [a_tail], wd_all, layer, x, 0.5, tm=DOWN_TM, tn=512,
                           next_gain=next_gain)


def kernel(x_prompt, x_sample, cache_moba_k, cache_moba_v, cache_swa_k, cache_swa_v, cache_mem_k, cache_mem_v, page_table, mem_prompt, g_ffn1, w_ffn1_gate, w_ffn1_up, w_ffn1_down, g_attn, w_in, w_out, sinks, g_mem, w_mem_kv, g_ffn2, w_ffn2_gate, w_ffn2_up, w_ffn2_down, g_final):
    bp, seq, d = x_prompt.shape
    bs = x_sample.shape[0]
    m = bp * seq
    n_pool = cache_moba_k.shape[1]
    mem = mem_prompt.reshape(bp * N_MEM, d)
    mem_k_cache = cache_mem_k.reshape(DEPTH, bs, N_MEM * N_MEM_HEADS, HEAD_DIM)
    mem_v_cache = cache_mem_v.reshape(DEPTH, bs, N_MEM * N_MEM_HEADS, HEAD_DIM)
    swa_k_cache = cache_swa_k.reshape(-1, bs, WINDOW * N_KV_HEADS, HEAD_DIM)
    swa_v_cache = cache_swa_v.reshape(-1, bs, WINDOW * N_KV_HEADS, HEAD_DIM)
    moba_k_flat = cache_moba_k.reshape(-1, n_pool, PAGE_ROWS, HEAD_DIM)

    moba_qkv, moba_ks, moba_vs = [], [], []
    swa_kp, swa_vp, swa_ks, swa_vs = [], [], [], []
    mem_kp, mem_vp = [], []
    x, xg, ssq = stream_start(x_prompt.reshape(m, d), x_sample.reshape(bs, d), g_ffn1, 0,
                              tm=DOWN_TM)
    for l in range(DEPTH):
        x, xg, ssq = _half_ffn(x, xg, ssq, (g_ffn1, l), w_ffn1_gate, w_ffn1_up, w_ffn1_down, l,
                               (g_attn, l))

        qkv = norm_project(xg, ssq, x, (g_attn, l), w_in, l, tm=DENSE_TM, tn=1024)
        qkv_s = qkv[m:]
        memkv = project(rms_norm(mem, g_mem, l, BF16), w_mem_kv, l, tn=512)
        mem_kp.append(memkv[:, :MQ_W].reshape(bp, N_MEM, N_MEM_HEADS, HEAD_DIM))
        mem_vp.append(memkv[:, MQ_W:].reshape(bp, N_MEM, N_MEM_HEADS, HEAD_DIM))

        qs = qkv_s[:, :Q_W]
        ks = qkv_s[:, Q_W:Q_W + KV_W]
        vs = qkv_s[:, Q_W + KV_W:Q_W + 2 * KV_W]
        qms = qkv_s[:, Q_W + 2 * KV_W:]
        ks4 = ks.reshape(bs, 1, N_KV_HEADS, HEAD_DIM)
        vs4 = vs.reshape(bs, 1, N_KV_HEADS, HEAD_DIM)

        oms = sample_memory_attention(qms.reshape(bs, 1, MQ_W), mem_k_cache, mem_v_cache, l)
        j = l // 2
        if l % 2 == 0:
            osp, omp = prompt_attention(qkv, memkv, bp, seq, None)
            sel = sample_moba_select(qs.reshape(bs, N_HEADS, HEAD_DIM), moba_k_flat, page_table, j)
            oss = sample_moba_attention(
                qs.reshape(bs, N_KV_HEADS, GROUP, HEAD_DIM),
                ks.reshape(bs, N_KV_HEADS, 1, HEAD_DIM), vs.reshape(bs, N_KV_HEADS, 1, HEAD_DIM),
                cache_moba_k, cache_moba_v, page_table, sel, j)
            moba_qkv.append(qkv)
            moba_ks.append(ks4)
            moba_vs.append(vs4)
        else:
            osp, omp = prompt_attention(qkv, memkv, bp, seq, sinks[j])
            oss = sample_swa_attention(qs.reshape(bs, 1, Q_W), ks.reshape(bs, 1, KV_W),
                                       vs.reshape(bs, 1, KV_W), swa_k_cache, swa_v_cache,
                                       sinks[j], j)
            last = jnp.stack([qkv[(b + 1) * seq - WINDOW:(b + 1) * seq, Q_W:Q_W + 2 * KV_W]
                              for b in range(bp)])
            swa_kp.append(last[:, :, :KV_W].reshape(bp, WINDOW, N_KV_HEADS, HEAD_DIM))
            swa_vp.append(last[:, :, KV_W:].reshape(bp, WINDOW, N_KV_HEADS, HEAD_DIM))
            swa_ks.append(jnp.concatenate([cache_swa_k[j], ks4], axis=1)[:, -WINDOW:])
            swa_vs.append(jnp.concatenate([cache_swa_v[j], vs4], axis=1)[:, -WINDOW:])

        x, xg, ssq = matmul_residual(
            [osp, omp], [oss.reshape(bs, Q_W), oms.reshape(bs, MQ_W)], w_out, l, x, 1.0,
            tm=DENSE_TM, tn=1024, next_gain=(g_ffn2, l))

        next_gain = (g_ffn1, l + 1) if l + 1 < DEPTH else None
        out = _half_ffn(x, xg, ssq, (g_ffn2, l), w_ffn2_gate, w_ffn2_up, w_ffn2_down, l,
                        next_gain)
        if next_gain is None:
            (x,) = out
        else:
            x, xg, ssq = out

    y_prompt, y_sample = stream_end(x, g_final, m, tm=DOWN_TM)
    moba_kp, moba_vp = export_prompt_kv(moba_qkv, m, tm=DOWN_TM)
    kv_shape = (len(moba_qkv), bp, seq, N_KV_HEADS, HEAD_DIM)
    return (y_prompt.reshape(bp, seq, d), y_sample.reshape(bs, 1, d),
            moba_kp.reshape(kv_shape), moba_vp.reshape(kv_shape),
            jnp.stack(moba_ks), jnp.stack(moba_vs),
            jnp.stack(swa_kp), jnp.stack(swa_vp), jnp.stack(swa_ks), jnp.stack(swa_vs),
            jnp.stack(mem_kp), jnp.stack(mem_vp))
```
